```python
import math
import jax, jax.numpy as jnp
from jax import lax
import numpy as np

D_MODEL = 1024
BATCH = 16
SEQ = 4096
DEPTH = 4
DEC_BATCH = 32
DEC_SEQ = 64
PAST_LEN = 4096

CHUNK = 64
Q_BLOCK = 128
GLA_HEADS = 4
GLA_DK = 64
GLA_DV = 128
GLA_RANK = 16
GLA_TAU = 16.0
DIFF_HEADS = 4
DIFF_DK = 64
DIFF_DV = 128
N_BUCKETS = 32
MAX_DISTANCE = 128
D_FF = 4 * D_MODEL
EPS = 1e-6

GLA_QK = GLA_HEADS * GLA_DK
GLA_V = GLA_HEADS * GLA_DV
DIFF_QK = DIFF_HEADS * 2 * DIFF_DK
DIFF_V = DIFF_HEADS * DIFF_DV
MIX_WIDTH = GLA_V + DIFF_V
SPLIT_SIZES = (GLA_QK, GLA_QK, GLA_V, GLA_V, GLA_RANK, DIFF_QK, DIFF_QK, DIFF_V)
IN_WIDTH = 2 * GLA_QK + 2 * GLA_V + GLA_RANK + 2 * DIFF_QK + DIFF_V

kernel_name = "hybrid_gla_diffattn_stream_step"


def rms_norm(x, g):
    xf = x.astype(jnp.float32)
    y = xf * lax.rsqrt(jnp.mean(xf * xf, axis=-1, keepdims=True) + EPS)
    return (y * g.astype(jnp.float32)).astype(x.dtype)


def t5_bucket(rel):
    nb = N_BUCKETS // 2
    ret = jnp.where(rel > 0, nb, 0)
    n = jnp.abs(rel)
    max_exact = nb // 2
    nf = jnp.maximum(n, 1).astype(jnp.float32)
    large = max_exact + (jnp.log(nf / max_exact) / math.log(MAX_DISTANCE / max_exact)
                         * (nb - max_exact)).astype(jnp.int32)
    large = jnp.minimum(large, nb - 1)
    return ret + jnp.where(n < max_exact, n, large)


def diff_attn_block(q, k, v, q_pos, k_pos, rel_bias, lam):
    logits = jnp.einsum('bqhmd,bkhmd->bhmqk', q, k,
                        preferred_element_type=jnp.float32) * (DIFF_DK ** -0.5)
    bias = rel_bias[t5_bucket(k_pos[None, :] - q_pos[:, None])]
    bias = jnp.transpose(bias, (2, 0, 1)).astype(jnp.float32)[None, :, None]
    visible = (k_pos[None, :] // CHUNK) <= (q_pos[:, None] // CHUNK)
    logits = jnp.where(visible, logits + bias, -jnp.inf)
    p = jax.nn.softmax(logits, axis=-1)
    w = p[:, :, 0] - lam * p[:, :, 1]
    return jnp.einsum('bhqk,bkhd->bqhd', w.astype(v.dtype), v)


def diff_attention(q, k, v, q_pos, k_pos, rel_bias, lam):
    B, Tq = q.shape[:2]
    if Tq <= Q_BLOCK:
        return diff_attn_block(q, k, v, q_pos, k_pos, rel_bias, lam)
    nblk = Tq // Q_BLOCK
    qb = jnp.moveaxis(q.reshape(B, nblk, Q_BLOCK, DIFF_HEADS, 2, DIFF_DK), 1, 0)
    pb = q_pos.reshape(nblk, Q_BLOCK)
    ob = lax.map(lambda a: diff_attn_block(a[0], k, v, a[1], k_pos, rel_bias, lam), (qb, pb))
    return jnp.moveaxis(ob, 0, 1).reshape(B, Tq, DIFF_HEADS, DIFF_DV)


def gla_chunk(S, inp):
    q, k, v, la = inp
    C = q.shape[1]
    b = jnp.cumsum(la, axis=1)
    causal = jnp.tril(jnp.ones((C, C), dtype=bool))
    diff = b[:, :, None] - b[:, None, :]
    decay = jnp.exp(jnp.where(causal[None, :, :, None, None], diff, -jnp.inf))
    A = jnp.einsum('bihd,bjhd,bijhd->bhij', q, k, decay)
    o = (jnp.einsum('bhij,bjhv->bihv', A, v)
         + jnp.einsum('bihd,bhdv->bihv', q * jnp.exp(b), S))
    b_last = b[:, -1]
    S = (jnp.exp(b_last)[..., None] * S
         + jnp.einsum('bjhd,bjhv->bhdv', k * jnp.exp(b_last[:, None] - b), v))
    return S, o


def gla(q, k, v, la, S0):
    B, T = q.shape[:2]
    C = min(CHUNK, T)
    n = T // C

    def to_chunks(a):
        return jnp.moveaxis(a.reshape(B, n, C, *a.shape[2:]), 1, 0)

    S, o = lax.scan(gla_chunk, S0, (to_chunks(q), to_chunks(k), to_chunks(v), to_chunks(la)))
    o = jnp.moveaxis(o, 0, 1).reshape(B, T, GLA_HEADS, GLA_DV)
    return o, S


def trunk_layer(x, mod, S0, k_past, v_past, q_pos, k_pos, lam_init,
                w_in_l, w_a2_l, b_a_l, gla_g_l, lq1, lk1, lq2, lk2, diff_g_l, rel_bias,
                w_out_l, n1_l, n2_l, w_up_l, w_down_l):
    B, T = x.shape[:2]
    sh1, sc1, g1, sh2, sc2, g2 = jnp.split(mod[:, None, :], 6, axis=-1)
    h = rms_norm(x, n1_l) * (1 + sc1) + sh1
    proj = h @ w_in_l
    idx = np.cumsum(SPLIT_SIZES)[:-1].tolist()
    qg, kg, vg, gg, ar, qd, kd, vd = jnp.split(proj, idx, axis=-1)
    f32 = jnp.float32
    qg = qg.reshape(B, T, GLA_HEADS, GLA_DK).astype(f32) * (GLA_DK ** -0.5)
    kg = kg.reshape(B, T, GLA_HEADS, GLA_DK).astype(f32)
    vg = vg.reshape(B, T, GLA_HEADS, GLA_DV).astype(f32)
    la = jax.nn.log_sigmoid((ar @ w_a2_l + b_a_l).astype(f32)) / GLA_TAU
    la = la.reshape(B, T, GLA_HEADS, GLA_DK)
    og, S = gla(qg, kg, vg, la, S0)
    og = rms_norm(og, gla_g_l) * jax.nn.silu(gg.astype(f32)).reshape(B, T, GLA_HEADS, GLA_DV)
    qd = qd.reshape(B, T, DIFF_HEADS, 2, DIFF_DK)
    kd = kd.reshape(B, T, DIFF_HEADS, 2, DIFF_DK)
    vd = vd.reshape(B, T, DIFF_HEADS, DIFF_DV)
    if k_past is None:
        k_all, v_all = kd, vd
    else:
        P = k_past.shape[1]
        k_all = jnp.concatenate([k_past.reshape(B, P, DIFF_HEADS, 2, DIFF_DK).astype(kd.dtype), kd], axis=1)
        v_all = jnp.concatenate([v_past.astype(vd.dtype), vd], axis=1)
    lam = (jnp.exp(jnp.sum(lq1.astype(f32) * lk1.astype(f32)))
           - jnp.exp(jnp.sum(lq2.astype(f32) * lk2.astype(f32))) + lam_init)
    od = diff_attention(qd, k_all, v_all, q_pos, k_pos, rel_bias, lam)
    od = rms_norm(od, diff_g_l) * (1.0 - lam_init)
    o = jnp.concatenate([og.reshape(B, T, GLA_V).astype(x.dtype),
                         od.reshape(B, T, DIFF_V).astype(x.dtype)], axis=-1)
    x = x + g1 * (o @ w_out_l)
    h2 = rms_norm(x, n2_l) * (1 + sc2) + sh2
    x = x + g2 * (jnp.square(jax.nn.relu(h2 @ w_up_l)) @ w_down_l)
    return x, S, kd.reshape(B, T, DIFF_HEADS, 2 * DIFF_DK), vd


def trunk(x, c, state0, cache_k, cache_v, w_mod, b_mod, norm1_g, w_in, w_a2, b_a, gla_norm_g,
          lam_q1, lam_k1, lam_q2, lam_k2, diff_norm_g, rel_bias, w_out, norm2_g, w_up, w_down, final_g):
    B, T = x.shape[:2]
    past_len = 0 if cache_k is None else cache_k.shape[2]
    q_pos = past_len + jnp.arange(T, dtype=jnp.int32)
    k_pos = jnp.arange(past_len + T, dtype=jnp.int32)
    c_act = jax.nn.silu(c)
    ks, vs, ss = [], [], []
    for l in range(DEPTH):
        lam_init = 0.8 - 0.6 * math.exp(-0.3 * l)
        mod = c_act @ w_mod[l] + b_mod[l]
        if state0 is None:
            S0 = jnp.zeros((B, GLA_HEADS, GLA_DK, GLA_DV), jnp.float32)
            kp, vp = None, None
        else:
            S0 = state0[l].astype(jnp.float32)
            kp, vp = cache_k[l], cache_v[l]
        x, S, kr, vr = trunk_layer(x, mod, S0, kp, vp, q_pos, k_pos, lam_init,
                                   w_in[l], w_a2[l], b_a[l], gla_norm_g[l],
                                   lam_q1[l], lam_k1[l], lam_q2[l], lam_k2[l], diff_norm_g[l], rel_bias,
                                   w_out[l], norm1_g[l], norm2_g[l], w_up[l], w_down[l])
        ks.append(kr)
        vs.append(vr)
        ss.append(S.astype(x.dtype))
    y = rms_norm(x, final_g)
    return y, jnp.stack(ks), jnp.stack(vs), jnp.stack(ss)


def setup_inputs(seed: int = 0) -> dict:
    key = jax.random.key(seed)
    ks = jax.random.split(key, 32)
    nrm = jax.random.normal
    f32 = jnp.float32
    D = D_MODEL
    return {
        "x_prompt": nrm(ks[0], (BATCH, SEQ, D), f32),
        "x_sample": nrm(ks[1], (DEC_BATCH, DEC_SEQ, D), f32),
        "cache_k": nrm(ks[2], (DEPTH, DEC_BATCH, PAST_LEN, DIFF_HEADS, 2 * DIFF_DK), f32),
        "cache_v": nrm(ks[3], (DEPTH, DEC_BATCH, PAST_LEN, DIFF_HEADS, DIFF_DV), f32),
        "state_gla": nrm(ks[4], (DEPTH, DEC_BATCH, GLA_HEADS, GLA_DK, GLA_DV), f32),
        "c_prompt": nrm(ks[5], (BATCH, D), f32),
        "c_sample": nrm(ks[6], (DEC_BATCH, D), f32),
        "w_mod": nrm(ks[7], (DEPTH, D, 6 * D), f32) * (0.5 * D ** -0.5),
        "b_mod": nrm(ks[8], (DEPTH, 6 * D), f32) * 0.01,
        "norm1_g": 1.0 + 0.02 * nrm(ks[9], (DEPTH, D), f32),
        "w_in": nrm(ks[10], (DEPTH, D, IN_WIDTH), f32) * D ** -0.5,
        "w_a2": nrm(ks[11], (DEPTH, GLA_RANK, GLA_QK), f32) * GLA_RANK ** -0.5,
        "b_a": nrm(ks[12], (DEPTH, GLA_QK), f32) * 0.1,
        "gla_norm_g": 1.0 + 0.02 * nrm(ks[13], (DEPTH, GLA_HEADS, GLA_DV), f32),
        "lam_q1": nrm(ks[14], (DEPTH, DIFF_DK), f32) * 0.1,
        "lam_k1": nrm(ks[15], (DEPTH, DIFF_DK), f32) * 0.1,
        "lam_q2": nrm(ks[16], (DEPTH, DIFF_DK), f32) * 0.1,
        "lam_k2": nrm(ks[17], (DEPTH, DIFF_DK), f32) * 0.1,
        "diff_norm_g": 1.0 + 0.02 * nrm(ks[18], (DEPTH, DIFF_HEADS, DIFF_DV), f32),
        "rel_bias": nrm(ks[19], (N_BUCKETS, DIFF_HEADS), f32) * 0.5,
        "w_out": nrm(ks[20], (DEPTH, MIX_WIDTH, D), f32) * MIX_WIDTH ** -0.5,
        "norm2_g": 1.0 + 0.02 * nrm(ks[21], (DEPTH, D), f32),
        "w_up": nrm(ks[22], (DEPTH, D, D_FF), f32) * D ** -0.5,
        "w_down": nrm(ks[23], (DEPTH, D_FF, D), f32) * D_FF ** -0.5,
        "final_g": 1.0 + 0.02 * nrm(ks[24], (D,), f32),
    }


def reference(x_prompt, x_sample, cache_k, cache_v, state_gla, c_prompt, c_sample,
              w_mod, b_mod, norm1_g, w_in, w_a2, b_a, gla_norm_g,
              lam_q1, lam_k1, lam_q2, lam_k2, diff_norm_g, rel_bias,
              w_out, norm2_g, w_up, w_down, final_g):
    y_prompt, new_k_prompt, new_v_prompt, gla_prompt = trunk(
        x_prompt, c_prompt, None, None, None, w_mod, b_mod, norm1_g, w_in, w_a2, b_a, gla_norm_g,
        lam_q1, lam_k1, lam_q2, lam_k2, diff_norm_g, rel_bias, w_out, norm2_g, w_up, w_down, final_g)
    y_sample, new_k_sample, new_v_sample, gla_sample = trunk(
        x_sample, c_sample, state_gla, cache_k, cache_v, w_mod, b_mod, norm1_g, w_in, w_a2, b_a, gla_norm_g,
        lam_q1, lam_k1, lam_q2, lam_k2, diff_norm_g, rel_bias, w_out, norm2_g, w_up, w_down, final_g)
    return (y_prompt, y_sample, new_k_prompt, new_v_prompt, gla_prompt,
            new_k_sample, new_v_sample, gla_sample)
```

```python
import functools
import math

import jax
import jax.numpy as jnp
import numpy as np
from jax import lax
from jax.experimental import pallas as pl
from jax.experimental.pallas import tpu as pltpu

F32 = jnp.float32
BF16 = jnp.bfloat16

D_MODEL = 1024
DEPTH = 4
CHUNK = 64
HEADS = 4
GLA_DK = 64
GLA_DV = 128
GLA_RANK = 16
GLA_TAU = 16.0
DIFF_DK = 64
DIFF_DV = 128
N_BUCKETS = 32
D_FF = 4 * D_MODEL
EPS = 1e-6

GLA_QK = HEADS * GLA_DK
GLA_V = HEADS * GLA_DV
DIFF_QK = HEADS * 2 * DIFF_DK
DIFF_V = HEADS * DIFF_DV
RANK_PAD = 128
PROJ_W = 2 * GLA_QK + 2 * GLA_V + 2 * DIFF_QK + DIFF_V + RANK_PAD

V7X_VMEM_LIMIT_BYTES = 60000 * 1024
ATT_BLOCK = 256
FAR_BUCKET = 15
GLA_FAST_DECAY_LIMIT = 30.0
CHUNK_SHIFT = 6
DK_SHIFT = 6
DV_SHIFT = 7


def _cparams(*sem):
    return pltpu.CompilerParams(dimension_semantics=sem, vmem_limit_bytes=V7X_VMEM_LIMIT_BYTES)


def _sigmoid(x):
    return 1.0 / (1.0 + jnp.exp(-x))


def _mod_kernel(c_ref, w_ref, b_ref, o_ref):
    c = c_ref[...]
    ca = (c * _sigmoid(c)).astype(BF16)
    o_ref[0] = jnp.dot(ca, w_ref[0].astype(BF16), preferred_element_type=F32) + b_ref[0]


def _modulation(c_all, w_mod, b_mod):
    nb = c_all.shape[0]
    ncol = 6 * D_MODEL // D_MODEL
    return pl.pallas_call(
        _mod_kernel,
        grid=(DEPTH, ncol),
        in_specs=[
            pl.BlockSpec((nb, D_MODEL), lambda l, j: (0, 0)),
            pl.BlockSpec((1, D_MODEL, D_MODEL), lambda l, j: (l, 0, j)),
            pl.BlockSpec((1, 1, D_MODEL), lambda l, j: (l, 0, j)),
        ],
        out_specs=pl.BlockSpec((1, nb, D_MODEL), lambda l, j: (l, 0, j)),
        out_shape=jax.ShapeDtypeStruct((DEPTH, nb, 6 * D_MODEL), F32),
        compiler_params=_cparams("arbitrary", "arbitrary"),
        name="adaln_modulation",
    )(c_all, w_mod, b_mod.reshape(DEPTH, 1, 6 * D_MODEL))


_BUCKET_THRESHOLDS = (12, 16, 23, 32, 46, 64, 91)


def _bias_kernel(rb_ref, o_ref):
    t = pl.program_id(0)
    h = pl.program_id(1)
    i = lax.broadcasted_iota(jnp.int32, (ATT_BLOCK, ATT_BLOCK), 0)
    j = lax.broadcasted_iota(jnp.int32, (ATT_BLOCK, ATT_BLOCK), 1)
    rel = j - i - t * ATT_BLOCK
    n = jnp.abs(rel)
    large = jnp.full_like(n, 8)
    for thr in _BUCKET_THRESHOLDS:
        large = large + jnp.where(n >= thr, 1, 0)
    bucket = jnp.where(rel > 0, N_BUCKETS // 2, 0) + jnp.where(n < 8, n, large)
    far = rb_ref[FAR_BUCKET, h]
    val = jnp.zeros((ATT_BLOCK, ATT_BLOCK), F32)
    for b in range(N_BUCKETS):
        val = jnp.where(bucket == b, rb_ref[b, h] - far, val)
    visible = (j >> CHUNK_SHIFT) <= (i >> CHUNK_SHIFT) + t * ATT_BLOCK
    o_ref[0, 0] = jnp.where(visible, val, -jnp.inf)


def _bias_tiles(rel_bias):
    return pl.pallas_call(
        _bias_kernel,
        grid=(2, HEADS),
        in_specs=[pl.BlockSpec(memory_space=pltpu.SMEM)],
        out_specs=pl.BlockSpec((1, 1, ATT_BLOCK, ATT_BLOCK), lambda t, h: (t, h, 0, 0)),
        out_shape=jax.ShapeDtypeStruct((2, HEADS, ATT_BLOCK, ATT_BLOCK), F32),
        compiler_params=_cparams("arbitrary", "arbitrary"),
        name="t5_bias_tiles",
    )(rel_bias)


def _in_proj_kernel(x_ref, mod_ref, n1_ref, w_ref, wa2_ref, ba_ref,
                    qg_ref, kg_ref, vg_ref, gg_ref, la_ref, qd_ref, kd_ref, vd_ref, kdb_ref, vdb_ref):
    nb, tt, _ = x_ref.shape
    x = x_ref[...]
    ms = jnp.mean(x * x, axis=-1, keepdims=True)
    y = x * lax.rsqrt(ms + EPS) * n1_ref[...]
    mod = mod_ref[...]
    h = y * (1.0 + mod[:, 1:2, :]) + mod[:, 0:1, :]
    hb = h.reshape(nb * tt, D_MODEL).astype(BF16)
    proj = jnp.dot(hb, w_ref[...], preferred_element_type=F32)

    def put(ref, lo, width, scale=None):
        v = proj[:, lo:lo + width]
        if scale is not None:
            v = v * scale
        ref[...] = v.reshape(nb, tt, width).astype(ref.dtype)
        return lo + width

    lo = put(qg_ref, 0, GLA_QK, GLA_DK ** -0.5)
    lo = put(kg_ref, lo, GLA_QK)
    lo = put(vg_ref, lo, GLA_V)
    lo = put(gg_ref, lo, GLA_V)
    lo = put(qd_ref, lo, DIFF_QK, DIFF_DK ** -0.5)
    put(kdb_ref, lo, DIFF_QK)
    lo = put(kd_ref, lo, DIFF_QK)
    put(vdb_ref, lo, DIFF_V)
    lo = put(vd_ref, lo, DIFF_V)
    ar = proj[:, lo:lo + RANK_PAD].astype(BF16)
    z = jnp.dot(ar, wa2_ref[...], preferred_element_type=F32) + ba_ref[...]
    la = (jnp.minimum(z, 0.0) - jnp.log(1.0 + jnp.exp(-jnp.abs(z)))) * (1.0 / GLA_TAU)
    la_ref[...] = la.reshape(nb, tt, GLA_QK)


def _row_tiling(batch, seq, target):
    if seq >= target:
        assert seq % target == 0
        return 1, target
    nb = min(batch, max(1, target // seq))
    while batch % nb:
        nb -= 1
    return nb, seq


def _in_proj(x, mod, n1, w_all, wa2p, ba):
    batch, seq, _ = x.shape
    nb, tt = _row_tiling(batch, seq, 512)
    grid = (batch // nb, seq // tt)
    row = lambda w: pl.BlockSpec((nb, tt, w), lambda b, t: (b, t, 0))
    const = lambda shape: pl.BlockSpec(shape, lambda b, t: (0,) * len(shape))
    widths = (GLA_QK, GLA_QK, GLA_V, GLA_V, GLA_QK, DIFF_QK, DIFF_QK, DIFF_V, DIFF_QK, DIFF_V)
    dtypes = (F32, F32, BF16, F32, F32, BF16, F32, F32, BF16, BF16)
    return pl.pallas_call(
        _in_proj_kernel,
        grid=grid,
        in_specs=[
            row(D_MODEL),
            pl.BlockSpec((nb, 6, D_MODEL), lambda b, t: (b, 0, 0)),
            const((1, 1, D_MODEL)),
            const((D_MODEL, PROJ_W)),
            const((RANK_PAD, GLA_QK)),
            const((1, GLA_QK)),
        ],
        out_specs=[row(w) for w in widths],
        out_shape=[jax.ShapeDtypeStruct((batch, seq, w), dt) for w, dt in zip(widths, dtypes)],
        compiler_params=_cparams("arbitrary", "arbitrary"),
        name="in_projection",
    )(x, mod, n1.reshape(1, 1, D_MODEL), w_all, wa2p, ba.reshape(1, GLA_QK))


def _stack_heads(a, width):
    return jnp.concatenate([a[:, h * width:(h + 1) * width] for h in range(HEADS)], axis=0)


def _unstack_heads(a, rows):
    return jnp.concatenate([a[h * rows:(h + 1) * rows, :] for h in range(HEADS)], axis=1)


def _gla_kernel(q_ref, k_ref, la_ref, v_ref, gg_ref, s0_ref, g_ref, og_ref, sout_ref, s_scr, b_scr, v_scr):
    t = pl.program_id(1)
    rows = q_ref.shape[1]
    nchunk = rows // CHUNK
    C = CHUNK

    @pl.when(t == 0)
    def _():
        s_scr[...] = s0_ref[0]

    r64 = lax.broadcasted_iota(jnp.int32, (C, C), 0)
    c64 = lax.broadcasted_iota(jnp.int32, (C, C), 1)
    tril = jnp.where(c64 <= r64, 1.0, 0.0).astype(BF16)
    rq = lax.broadcasted_iota(jnp.int32, (HEADS * C, GLA_QK), 0)
    cq = lax.broadcasted_iota(jnp.int32, (HEADS * C, GLA_QK), 1)
    head_lanes = (rq >> CHUNK_SHIFT) == (cq >> DK_SHIFT)
    ra = lax.broadcasted_iota(jnp.int32, (HEADS * C, C), 0)
    ca = lax.broadcasted_iota(jnp.int32, (HEADS * C, C), 1)
    causal = ca <= (ra & (C - 1))
    row_i = lax.broadcasted_iota(jnp.int32, (C, GLA_QK), 0)
    re = lax.broadcasted_iota(jnp.int32, (GLA_QK, GLA_V), 0)
    ce = lax.broadcasted_iota(jnp.int32, (GLA_QK, GLA_V), 1)
    expand = jnp.where((re >> DK_SHIFT) == (ce >> DV_SHIFT), 1.0, 0.0).astype(BF16)

    def chunk_body(c, carry):
        base = pl.multiple_of(c * C, C)
        q = q_ref[0, pl.ds(base, C), :]
        k = k_ref[0, pl.ds(base, C), :]
        la = la_ref[0, pl.ds(base, C), :]
        v = v_ref[0, pl.ds(base, C), :]
        la1 = la.astype(BF16)
        r1 = la - la1.astype(F32)
        la2 = r1.astype(BF16)
        la3 = (r1 - la2.astype(F32)).astype(BF16)
        b = (jnp.dot(tril, la1, preferred_element_type=F32)
             + jnp.dot(tril, la2, preferred_element_type=F32)
             + jnp.dot(tril, la3, preferred_element_type=F32))
        b_last = b[C - 1:C, :]
        fast = jnp.min(b_last) >= -GLA_FAST_DECAY_LIMIT

        qt = q * jnp.exp(b)
        ks = (k * jnp.exp(b_last - b)).astype(BF16)
        s_old = s_scr[...]
        qstack = jnp.where(head_lanes, jnp.concatenate([qt] * HEADS, axis=0), 0.0).astype(BF16)
        o = jnp.dot(qstack, s_old.astype(BF16), preferred_element_type=F32)

        ds_full = lax.dot_general(ks, v, (((0,), (0,)), ((), ())), preferred_element_type=F32)
        ds = jnp.concatenate(
            [ds_full[h * GLA_DK:(h + 1) * GLA_DK, h * GLA_DV:(h + 1) * GLA_DV] for h in range(HEADS)], axis=0)
        decay = jnp.transpose(jnp.broadcast_to(jnp.exp(b_last), (GLA_DV, GLA_QK)))
        s_scr[...] = decay * s_old + ds

        def intra_fast(_):
            kt = (k * jnp.exp(-b)).astype(BF16)
            a = lax.dot_general(qstack, kt, (((1,), (1,)), ((), ())), preferred_element_type=F32)
            a = jnp.where(causal, a, 0.0).astype(BF16)
            return jnp.concatenate(
                [jnp.dot(a[h * C:(h + 1) * C, :], v[:, h * GLA_DV:(h + 1) * GLA_DV],
                         preferred_element_type=F32) for h in range(HEADS)], axis=0)

        def intra_exact(_):
            b_scr[...] = b
            v_scr[...] = v.astype(F32)

            def key_row(j, acc):
                kj = k_ref[0, pl.ds(base + j, 1), :]
                bj = b_scr[pl.ds(j, 1), :]
                vj = v_scr[pl.ds(j, 1), :]
                w = jnp.where(row_i >= j, jnp.exp(jnp.minimum(b - bj, 0.0)), 0.0) * q * kj
                return acc + jnp.dot(w.astype(BF16), expand, preferred_element_type=F32) * vj

            acc = lax.fori_loop(0, C, key_row, jnp.zeros((C, GLA_V), F32))
            return _stack_heads(acc, GLA_DV)

        o = o + lax.cond(fast, intra_fast, intra_exact, 0)
        og = o * lax.rsqrt(jnp.mean(o * o, axis=-1, keepdims=True) + EPS) * g_ref[...]
        gate = _stack_heads(gg_ref[0, pl.ds(base, C), :], GLA_DV)
        og = og * (gate * _sigmoid(gate))
        og_ref[0, pl.ds(base, C), :] = _unstack_heads(og, C).astype(og_ref.dtype)
        return carry

    lax.fori_loop(0, nchunk, chunk_body, 0)

    @pl.when(t == pl.num_programs(1) - 1)
    def _():
        sout_ref[0] = s_scr[...]


def _gla(qg, kg, la, vg, gg, s0, g_stack):
    batch, seq, _ = qg.shape
    rows = min(seq, 512)
    assert seq % rows == 0 and rows % CHUNK == 0
    grid = (batch, seq // rows)
    row = lambda w: pl.BlockSpec((1, rows, w), lambda b, t: (b, t, 0))
    state = pl.BlockSpec((1, GLA_QK, GLA_DV), lambda b, t: (b, 0, 0))
    return pl.pallas_call(
        _gla_kernel,
        grid=grid,
        in_specs=[row(GLA_QK), row(GLA_QK), row(GLA_QK), row(GLA_V), row(GLA_V), state,
                  pl.BlockSpec((GLA_QK, GLA_DV), lambda b, t: (0, 0))],
        out_specs=[row(GLA_V), state],
        out_shape=[jax.ShapeDtypeStruct((batch, seq, GLA_V), BF16),
                   jax.ShapeDtypeStruct((batch, GLA_QK, GLA_DV), F32)],
        scratch_shapes=[pltpu.VMEM((GLA_QK, GLA_DV), F32),
                        pltpu.VMEM((CHUNK, GLA_QK), F32),
                        pltpu.VMEM((CHUNK, GLA_V), F32)],
        compiler_params=_cparams("arbitrary", "arbitrary"),
        name="gla_chunks",
    )(qg, kg, la, vg, gg, s0, g_stack)


def _split_maps(qh):
    lane = lax.broadcasted_iota(jnp.int32, qh.shape, 1)
    zero = jnp.zeros_like(qh)
    return jnp.concatenate([jnp.where(lane < DIFF_DK, qh, zero), jnp.where(lane >= DIFF_DK, qh, zero)], axis=0)


def _flash_update(q2, kb, vb, bias, m_ref, l_ref, acc_ref):
    s = lax.dot_general(q2, kb, (((1,), (1,)), ((), ())), preferred_element_type=F32)
    if bias is not None:
        s = s + bias
    m_prev = m_ref[...]
    m_next = jnp.maximum(m_prev, jnp.max(s, axis=1, keepdims=True))
    p = jnp.exp(s - m_next[:, :1])
    alpha = jnp.exp(m_prev - m_next)
    l_ref[...] = alpha * l_ref[...] + jnp.sum(p, axis=1, keepdims=True)
    acc_ref[...] = alpha * acc_ref[...] + jnp.dot(p.astype(BF16), vb, preferred_element_type=F32)
    m_ref[...] = m_next


def _flash_init(m_ref, l_ref, acc_ref):
    m_ref[...] = jnp.full(m_ref.shape, -jnp.inf, F32)
    l_ref[...] = jnp.zeros(l_ref.shape, F32)
    acc_ref[...] = jnp.zeros(acc_ref.shape, F32)


def _diff_lambda(lam_ref, lam_init):
    lv = lam_ref[...]
    s1 = jnp.sum(lv[0:1, :] * lv[1:2, :], axis=1, keepdims=True)
    s2 = jnp.sum(lv[2:3, :] * lv[3:4, :], axis=1, keepdims=True)
    return jnp.exp(s1) - jnp.exp(s2) + lam_init


def _diff_finish(tq, lam, lam_init, g, l_ref, acc_ref):
    out = acc_ref[...] / l_ref[...]
    od = out[:tq, :] - lam * out[tq:, :]
    od = od * lax.rsqrt(jnp.mean(od * od, axis=-1, keepdims=True) + EPS) * g
    return od * (1.0 - lam_init)


def _attn_prompt_kernel(lam_init, q_ref, k_ref, v_ref, bias_ref, lam_ref, g_ref, o_ref, m_ref, l_ref, acc_ref):
    qi = pl.program_id(1)
    tq = q_ref.shape[1]
    lam = _diff_lambda(lam_ref, lam_init)
    for h in range(HEADS):
        cols = slice(h * 2 * DIFF_DK, (h + 1) * 2 * DIFF_DK)
        vcols = slice(h * DIFF_DV, (h + 1) * DIFF_DV)
        q2 = _split_maps(q_ref[0, :, cols])
        _flash_init(m_ref, l_ref, acc_ref)

        def far_block(kb, carry):
            base = pl.multiple_of(kb * ATT_BLOCK, ATT_BLOCK)
            _flash_update(q2, k_ref[0, pl.ds(base, ATT_BLOCK), cols], v_ref[0, pl.ds(base, ATT_BLOCK), vcols],
                          None, m_ref, l_ref, acc_ref)
            return carry

        lax.fori_loop(0, jnp.maximum(qi - 1, 0), far_block, 0)

        @pl.when(qi >= 1)
        def _():
            base = pl.multiple_of((qi - 1) * ATT_BLOCK, ATT_BLOCK)
            bias = jnp.concatenate([bias_ref[1, h]] * 2, axis=0)
            _flash_update(q2, k_ref[0, pl.ds(base, ATT_BLOCK), cols], v_ref[0, pl.ds(base, ATT_BLOCK), vcols],
                          bias, m_ref, l_ref, acc_ref)

        base = pl.multiple_of(qi * ATT_BLOCK, ATT_BLOCK)
        bias = jnp.concatenate([bias_ref[0, h]] * 2, axis=0)
        _flash_update(q2, k_ref[0, pl.ds(base, ATT_BLOCK), cols], v_ref[0, pl.ds(base, ATT_BLOCK), vcols],
                      bias, m_ref, l_ref, acc_ref)
        o_ref[0, :, vcols] = _diff_finish(tq, lam, lam_init, g_ref[:, vcols], l_ref, acc_ref).astype(o_ref.dtype)


def _attn_prompt(qd, kdb, vdb, bias, lam_vecs, g_row, lam_init):
    batch, seq, _ = qd.shape
    tq = ATT_BLOCK
    assert seq % tq == 0
    full = lambda w: pl.BlockSpec((1, seq, w), lambda b, i: (b, 0, 0))
    return pl.pallas_call(
        functools.partial(_attn_prompt_kernel, lam_init),
        grid=(batch, seq // tq),
        in_specs=[
            pl.BlockSpec((1, tq, DIFF_QK), lambda b, i: (b, i, 0)),
            full(DIFF_QK), full(DIFF_V),
            pl.BlockSpec((2, HEADS, ATT_BLOCK, ATT_BLOCK), lambda b, i: (0, 0, 0, 0)),
            pl.BlockSpec((4, DIFF_DK), lambda b, i: (0, 0)),
            pl.BlockSpec((1, DIFF_V), lambda b, i: (0, 0)),
        ],
        out_specs=pl.BlockSpec((1, tq, DIFF_V), lambda b, i: (b, i, 0)),
        out_shape=jax.ShapeDtypeStruct((batch, seq, DIFF_V), BF16),
        scratch_shapes=[pltpu.VMEM((2 * tq, DIFF_DV), F32)] * 3,
        compiler_params=_cparams("arbitrary", "arbitrary"),
        name="diff_attention_prompt",
    )(qd, kdb, vdb, bias, lam_vecs, g_row)


def _attn_sample_kernel(lam_init, q_ref, ck_ref, cv_ref, nk_ref, nv_ref, bias_ref, lam_ref, g_ref, o_ref,
                        m_ref, l_ref, acc_ref):
    step = pl.program_id(1)
    nstep = pl.num_programs(1)
    tq = q_ref.shape[1]
    tk = ck_ref.shape[1]
    nsub = tk // ATT_BLOCK

    @pl.when(step == 0)
    def _():
        _flash_init(m_ref, l_ref, acc_ref)

    def head_refs(h):
        return m_ref.at[h], l_ref.at[h], acc_ref.at[h]

    def cache_blocks(with_bias):
        for h in range(HEADS):
            cols = slice(h * 2 * DIFF_DK, (h + 1) * 2 * DIFF_DK)
            vcols = slice(h * DIFF_DV, (h + 1) * DIFF_DV)
            q2 = _split_maps(q_ref[0, :, cols])
            for sb in range(nsub):
                rows = slice(sb * ATT_BLOCK, (sb + 1) * ATT_BLOCK)
                bias = None
                if with_bias and sb == nsub - 1:
                    bias = jnp.concatenate([bias_ref[1, h, :tq, :]] * 2, axis=0)
                _flash_update(q2, ck_ref[0, rows, cols].astype(BF16), cv_ref[0, rows, vcols].astype(BF16),
                              bias, *head_refs(h))

    @pl.when(step < nstep - 1)
    def _():
        cache_blocks(False)

    @pl.when(step == nstep - 1)
    def _():
        cache_blocks(True)
        lam = _diff_lambda(lam_ref, lam_init)
        for h in range(HEADS):
            cols = slice(h * 2 * DIFF_DK, (h + 1) * 2 * DIFF_DK)
            vcols = slice(h * DIFF_DV, (h + 1) * DIFF_DV)
            q2 = _split_maps(q_ref[0, :, cols])
            bias = jnp.concatenate([bias_ref[0, h, :tq, :tq]] * 2, axis=0)
            mh, lh, ah = head_refs(h)
            _flash_update(q2, nk_ref[0, :, cols], nv_ref[0, :, vcols], bias, mh, lh, ah)
            o_ref[0, :, vcols] = _diff_finish(tq, lam, lam_init, g_ref[:, vcols], lh, ah).astype(o_ref.dtype)


def _attn_sample(qd, cache_k, cache_v, kdb, vdb, bias, lam_vecs, g_row, lam_init):
    batch, tq, _ = qd.shape
    past = cache_k.shape[1]
    assert tq == CHUNK and past % CHUNK == 0, "sample queries must be one chunk, appended at a chunk boundary"
    tk = min(past, 1024)
    assert past % tk == 0 and tk % ATT_BLOCK == 0
    new = lambda w: pl.BlockSpec((1, tq, w), lambda b, s: (b, 0, 0))
    blk = lambda w: pl.BlockSpec((1, tk, w), lambda b, s: (b, s, 0))
    return pl.pallas_call(
        functools.partial(_attn_sample_kernel, lam_init),
        grid=(batch, past // tk),
        in_specs=[
            new(DIFF_QK), blk(DIFF_QK), blk(DIFF_V), new(DIFF_QK), new(DIFF_V),
            pl.BlockSpec((2, HEADS, ATT_BLOCK, ATT_BLOCK), lambda b, s: (0, 0, 0, 0)),
            pl.BlockSpec((4, DIFF_DK), lambda b, s: (0, 0)),
            pl.BlockSpec((1, DIFF_V), lambda b, s: (0, 0)),
        ],
        out_specs=new(DIFF_V),
        out_shape=jax.ShapeDtypeStruct((batch, tq, DIFF_V), BF16),
        scratch_shapes=[pltpu.VMEM((HEADS, 2 * tq, DIFF_DV), F32)] * 3,
        compiler_params=_cparams("arbitrary", "arbitrary"),
        name="diff_attention_sample",
    )(qd, cache_k, cache_v, kdb, vdb, bias, lam_vecs, g_row)


def _out_mlp_kernel(final, x_ref, og_ref, od_ref, mod_ref, n2_ref, fg_ref, wo_ref, wu_ref, wd_ref, o_ref):
    nb, tt, _ = x_ref.shape
    rows = nb * tt
    mod = mod_ref[...]
    o = jnp.concatenate([og_ref[...].reshape(rows, GLA_V), od_ref[...].reshape(rows, DIFF_V)], axis=1)
    attn = jnp.dot(o, wo_ref[...], preferred_element_type=F32).reshape(nb, tt, D_MODEL)
    x1 = x_ref[...] + mod[:, 2:3, :] * attn
    ms = jnp.mean(x1 * x1, axis=-1, keepdims=True)
    h2 = x1 * lax.rsqrt(ms + EPS) * n2_ref[...]
    h2 = (h2 * (1.0 + mod[:, 4:5, :]) + mod[:, 3:4, :]).reshape(rows, D_MODEL).astype(BF16)
    ff = jnp.zeros((rows, D_MODEL), F32)
    for c in range(D_FF // D_MODEL):
        cs = slice(c * D_MODEL, (c + 1) * D_MODEL)
        u = jnp.maximum(jnp.dot(h2, wu_ref[:, cs], preferred_element_type=F32), 0.0)
        ff = ff + jnp.dot((u * u).astype(BF16), wd_ref[cs, :], preferred_element_type=F32)
    x2 = x1 + mod[:, 5:6, :] * ff.reshape(nb, tt, D_MODEL)
    if final:
        ms2 = jnp.mean(x2 * x2, axis=-1, keepdims=True)
        x2 = x2 * lax.rsqrt(ms2 + EPS) * fg_ref[...]
    o_ref[...] = x2


def _out_mlp(x, og, od, mod, n2, final_g, w_out, w_up, w_down, final):
    batch, seq, _ = x.shape
    nb, tt = _row_tiling(batch, seq, 512)
    row = lambda w: pl.BlockSpec((nb, tt, w), lambda b, t: (b, t, 0))
    const = lambda shape: pl.BlockSpec(shape, lambda b, t: (0,) * len(shape), pipeline_mode=pl.Buffered(1))
    return pl.pallas_call(
        functools.partial(_out_mlp_kernel, final),
        grid=(batch // nb, seq // tt),
        in_specs=[
            row(D_MODEL), row(GLA_V), row(DIFF_V),
            pl.BlockSpec((nb, 6, D_MODEL), lambda b, t: (b, 0, 0)),
            const((1, 1, D_MODEL)), const((1, 1, D_MODEL)),
            const((D_MODEL, D_MODEL)), const((D_MODEL, D_FF)), const((D_FF, D_MODEL)),
        ],
        out_specs=row(D_MODEL),
        out_shape=jax.ShapeDtypeStruct((batch, seq, D_MODEL), F32),
        compiler_params=_cparams("arbitrary", "arbitrary"),
        name="out_proj_mlp",
    )(x, og, od, mod, n2.reshape(1, 1, D_MODEL), final_g.reshape(1, 1, D_MODEL), w_out, w_up, w_down)


def _pack_w_in(w_in_l):
    a = 2 * GLA_QK + 2 * GLA_V
    main = jnp.concatenate([w_in_l[:, :a], w_in_l[:, a + GLA_RANK:]], axis=1)
    ar = jnp.pad(w_in_l[:, a:a + GLA_RANK], ((0, 0), (0, RANK_PAD - GLA_RANK)))
    return jnp.concatenate([main, ar], axis=1).astype(BF16)


def _trunk(x, mod_all, state0, cache_k, cache_v, params, bias):
    batch, seq, _ = x.shape
    ks, vs, ss = [], [], []
    for l in range(DEPTH):
        p = params[l]
        lam_init = 0.8 - 0.6 * math.exp(-0.3 * l)
        mod = mod_all[l]
        qg, kg, vg, gg, la, qd, kd, vd, kdb, vdb = _in_proj(x, mod, p["n1"], p["w_all"], p["wa2p"], p["ba"])
        if state0 is None:
            s0 = jnp.zeros((batch, GLA_QK, GLA_DV), F32)
        else:
            s0 = state0[l].reshape(batch, GLA_QK, GLA_DV)
        og, s_new = _gla(qg, kg, la, vg, gg, s0, p["g_stack"])
        if cache_k is None:
            od = _attn_prompt(qd, kdb, vdb, bias, p["lam_vecs"], p["diff_g"], lam_init)
        else:
            past = cache_k.shape[2]
            od = _attn_sample(qd, cache_k[l].reshape(batch, past, DIFF_QK), cache_v[l].reshape(batch, past, DIFF_V),
                              kdb, vdb, bias, p["lam_vecs"], p["diff_g"], lam_init)
        x = _out_mlp(x, og, od, mod, p["n2"], p["final_g"], p["w_out"], p["w_up"], p["w_down"], l == DEPTH - 1)
        ks.append(kd.reshape(batch, seq, HEADS, 2 * DIFF_DK))
        vs.append(vd.reshape(batch, seq, HEADS, DIFF_DV))
        ss.append(s_new.reshape(batch, HEADS, GLA_DK, GLA_DV))
    return x, jnp.stack(ks), jnp.stack(vs), jnp.stack(ss)


def kernel(x_prompt, x_sample, cache_k, cache_v, state_gla, c_prompt, c_sample, w_mod, b_mod, norm1_g, w_in, w_a2, b_a, gla_norm_g, lam_q1, lam_k1, lam_q2, lam_k2, diff_norm_g, rel_bias, w_out, norm2_g, w_up, w_down, final_g):
    nbp = x_prompt.shape[0]
    mod_all = _modulation(jnp.concatenate([c_prompt, c_sample], axis=0), w_mod, b_mod)
    mod_all = mod_all.reshape(DEPTH, -1, 6, D_MODEL)
    bias = _bias_tiles(rel_bias)
    params = []
    for l in range(DEPTH):
        params.append(dict(
            n1=norm1_g[l], n2=norm2_g[l], final_g=final_g,
            w_all=_pack_w_in(w_in[l]),
            wa2p=jnp.pad(w_a2[l], ((0, RANK_PAD - GLA_RANK), (0, 0))).astype(BF16),
            ba=b_a[l],
            g_stack=jnp.repeat(gla_norm_g[l], CHUNK, axis=0),
            lam_vecs=jnp.stack([lam_q1[l], lam_k1[l], lam_q2[l], lam_k2[l]]),
            diff_g=diff_norm_g[l].reshape(1, DIFF_V),
            w_out=w_out[l].astype(BF16), w_up=w_up[l].astype(BF16), w_down=w_down[l].astype(BF16),
        ))
    y_p, k_p, v_p, s_p = _trunk(x_prompt, mod_all[:, :nbp], None, None, None, params, bias)
    y_s, k_s, v_s, s_s = _trunk(x_sample, mod_all[:, nbp:], state_gla, cache_k, cache_v, params, bias)
    return (y_p, y_s, k_p, v_p, s_p, k_s, v_s, s_s)
```

```python
import functools
import math

import jax
import jax.numpy as jnp
import numpy as np
from jax import lax
from jax.experimental import pallas as pl
from jax.experimental.pallas import tpu as pltpu

F32 = jnp.float32
BF16 = jnp.bfloat16

D_MODEL = 1024
DEPTH = 4
CHUNK = 64
HEADS = 4
GLA_DK = 64
GLA_DV = 128
GLA_RANK = 16
GLA_TAU = 16.0
DIFF_DK = 64
DIFF_DV = 128
N_BUCKETS = 32
D_FF = 4 * D_MODEL
EPS = 1e-6

GLA_QK = HEADS * GLA_DK
GLA_V = HEADS * GLA_DV
DIFF_QK = HEADS * 2 * DIFF_DK
DIFF_V = HEADS * DIFF_DV
RANK_PAD = 128
PROJ_W = 2 * GLA_QK + 2 * GLA_V + 2 * DIFF_QK + DIFF_V + RANK_PAD

V7X_VMEM_LIMIT_BYTES = 60000 * 1024
ATT_BLOCK = 256
FAR_BUCKET = 15
GLA_FAST_DECAY_LIMIT = 30.0
CHUNK_SHIFT = 6
DK_SHIFT = 6
DV_SHIFT = 7


def _cparams(*sem):
    return pltpu.CompilerParams(dimension_semantics=sem, vmem_limit_bytes=V7X_VMEM_LIMIT_BYTES)


def _sigmoid(x):
    return 1.0 / (1.0 + jnp.exp(-x))


def _mod_kernel(c_ref, w_ref, b_ref, o_ref):
    c = c_ref[...]
    ca = (c * _sigmoid(c)).astype(BF16)
    o_ref[0] = jnp.dot(ca, w_ref[0].astype(BF16), preferred_element_type=F32) + b_ref[0]


def _modulation(c_all, w_mod, b_mod):
    nb = c_all.shape[0]
    ncol = 6 * D_MODEL // D_MODEL
    return pl.pallas_call(
        _mod_kernel,
        grid=(DEPTH, ncol),
        in_specs=[
            pl.BlockSpec((nb, D_MODEL), lambda l, j: (0, 0)),
            pl.BlockSpec((1, D_MODEL, D_MODEL), lambda l, j: (l, 0, j)),
            pl.BlockSpec((1, 1, D_MODEL), lambda l, j: (l, 0, j)),
        ],
        out_specs=pl.BlockSpec((1, nb, D_MODEL), lambda l, j: (l, 0, j)),
        out_shape=jax.ShapeDtypeStruct((DEPTH, nb, 6 * D_MODEL), F32),
        compiler_params=_cparams("arbitrary", "arbitrary"),
        name="adaln_modulation",
    )(c_all, w_mod, b_mod.reshape(DEPTH, 1, 6 * D_MODEL))


_BUCKET_THRESHOLDS = (12, 16, 23, 32, 46, 64, 91)


def _bias_kernel(rb_ref, o_ref):
    orient = pl.program_id(0)
    t = pl.program_id(1)
    h = pl.program_id(2)
    r = lax.broadcasted_iota(jnp.int32, (ATT_BLOCK, ATT_BLOCK), 0)
    c = lax.broadcasted_iota(jnp.int32, (ATT_BLOCK, ATT_BLOCK), 1)
    i = r + orient * (c - r)
    j = c + orient * (r - c)
    rel = j - i - t * ATT_BLOCK
    n = jnp.abs(rel)
    large = jnp.full_like(n, 8)
    for thr in _BUCKET_THRESHOLDS:
        large = large + jnp.where(n >= thr, 1, 0)
    bucket = jnp.where(rel > 0, N_BUCKETS // 2, 0) + jnp.where(n < 8, n, large)
    far = rb_ref[FAR_BUCKET, h]
    val = jnp.zeros((ATT_BLOCK, ATT_BLOCK), F32)
    for b in range(N_BUCKETS):
        val = jnp.where(bucket == b, rb_ref[b, h] - far, val)
    visible = (j >> CHUNK_SHIFT) <= (i >> CHUNK_SHIFT) + t * ATT_BLOCK
    o_ref[0, 0, 0] = jnp.where(visible, val, -jnp.inf)


def _bias_tiles(rel_bias):
    return pl.pallas_call(
        _bias_kernel,
        grid=(2, 2, HEADS),
        in_specs=[pl.BlockSpec(memory_space=pltpu.SMEM)],
        out_specs=pl.BlockSpec((1, 1, 1, ATT_BLOCK, ATT_BLOCK), lambda o, t, h: (o, t, h, 0, 0)),
        out_shape=jax.ShapeDtypeStruct((2, 2, HEADS, ATT_BLOCK, ATT_BLOCK), F32),
        compiler_params=_cparams("arbitrary", "arbitrary", "arbitrary"),
        name="t5_bias_tiles",
    )(rel_bias)


def _in_proj_kernel(n_alias, transposed_v, x_ref, mod_ref, n1_ref, w_ref, wa2_ref, ba_ref, *refs):
    qg_ref, kg_ref, vg_ref, gg_ref, la_ref, qd_ref, kd_ref, vd_ref, kdb_ref, vdb_ref = refs[n_alias:]
    nb, tt, _ = x_ref.shape
    x = x_ref[...]
    ms = jnp.mean(x * x, axis=-1, keepdims=True)
    y = x * lax.rsqrt(ms + EPS) * n1_ref[...]
    mod = mod_ref[...]
    h = y * (1.0 + mod[:, 1:2, :]) + mod[:, 0:1, :]
    hb = h.reshape(nb * tt, D_MODEL).astype(BF16)
    proj = jnp.dot(hb, w_ref[...], preferred_element_type=F32)

    def put(ref, lo, width, scale=None):
        v = proj[:, lo:lo + width]
        if scale is not None:
            v = v * scale
        ref[...] = v.reshape(nb, tt, width).astype(ref.dtype)
        return lo + width

    lo = put(qg_ref, 0, GLA_QK, GLA_DK ** -0.5)
    lo = put(kg_ref, lo, GLA_QK)
    lo = put(vg_ref, lo, GLA_V)
    lo = put(gg_ref, lo, GLA_V)
    lo = put(qd_ref, lo, DIFF_QK, DIFF_DK ** -0.5)

    def put_cache(ref, lo):
        for h in range(HEADS):
            slab = proj[:, lo + h * DIFF_DV:lo + (h + 1) * DIFF_DV]
            ref[0, :, :, h, :] = slab.reshape(nb, tt, DIFF_DV)

    put_cache(kd_ref, lo)
    lo = put(kdb_ref, lo, DIFF_QK)
    put_cache(vd_ref, lo)
    if transposed_v:
        vdb_ref[0] = jnp.transpose(proj[:, lo:lo + DIFF_V]).astype(vdb_ref.dtype)
        lo = lo + DIFF_V
    else:
        lo = put(vdb_ref, lo, DIFF_V)
    ar = proj[:, lo:lo + RANK_PAD].astype(BF16)
    z = jnp.dot(ar, wa2_ref[...], preferred_element_type=F32) + ba_ref[...]
    la = (jnp.minimum(z, 0.0) - jnp.log(1.0 + jnp.exp(-jnp.abs(z)))) * (1.0 / GLA_TAU)
    la_ref[...] = la.reshape(nb, tt, GLA_QK)


def _row_tiling(batch, seq, target):
    if seq >= target:
        assert seq % target == 0
        return 1, target
    nb = min(batch, max(1, target // seq))
    while batch % nb:
        nb -= 1
    return nb, seq


def _in_proj(x, mod, n1, w_all, wa2p, ba, layer, new_k, new_v, transposed_v):
    batch, seq, _ = x.shape
    nb, tt = (1, min(seq, 512)) if transposed_v else _row_tiling(batch, seq, 512)
    assert seq % tt == 0
    grid = (batch // nb, seq // tt)
    row = lambda w: pl.BlockSpec((nb, tt, w), lambda b, t: (b, t, 0))
    const = lambda shape: pl.BlockSpec(shape, lambda b, t: (0,) * len(shape))
    cache = pl.BlockSpec((1, nb, tt, HEADS, DIFF_DV), lambda b, t: (layer, b, t, 0, 0))
    cache_shape = jax.ShapeDtypeStruct((DEPTH, batch, seq, HEADS, DIFF_DV), F32)
    sds = lambda w, dt: jax.ShapeDtypeStruct((batch, seq, w), dt)
    if transposed_v:
        vb_spec = pl.BlockSpec((1, DIFF_V, tt), lambda b, t: (b, 0, t))
        vb_shape = jax.ShapeDtypeStruct((batch, DIFF_V, seq), BF16)
    else:
        vb_spec, vb_shape = row(DIFF_V), sds(DIFF_V, BF16)
    out_specs = [row(GLA_QK), row(GLA_QK), row(GLA_V), row(GLA_V), row(GLA_QK), row(DIFF_QK),
                 cache, cache, row(DIFF_QK), vb_spec]
    out_shape = [sds(GLA_QK, F32), sds(GLA_QK, F32), sds(GLA_V, BF16), sds(GLA_V, F32), sds(GLA_QK, F32),
                 sds(DIFF_QK, BF16), cache_shape, cache_shape, sds(DIFF_QK, BF16), vb_shape]
    in_specs = [
        row(D_MODEL),
        pl.BlockSpec((nb, 6, D_MODEL), lambda b, t: (b, 0, 0)),
        const((1, 1, D_MODEL)),
        const((D_MODEL, PROJ_W)),
        const((RANK_PAD, GLA_QK)),
        const((1, GLA_QK)),
    ]
    args = [x, mod, n1.reshape(1, 1, D_MODEL), w_all, wa2p, ba.reshape(1, GLA_QK)]
    aliases = {}
    if new_k is not None:
        aliases = {len(args): 6, len(args) + 1: 7}
        in_specs += [pl.BlockSpec(memory_space=pl.ANY)] * 2
        args += [new_k, new_v]
    return pl.pallas_call(
        functools.partial(_in_proj_kernel, len(aliases), transposed_v),
        grid=grid,
        in_specs=in_specs,
        out_specs=out_specs,
        out_shape=out_shape,
        input_output_aliases=aliases,
        compiler_params=_cparams("arbitrary", "arbitrary"),
        name="in_projection",
    )(*args)


def _stack_heads(a, width):
    return jnp.concatenate([a[:, h * width:(h + 1) * width] for h in range(HEADS)], axis=0)


def _unstack_heads(a, rows):
    return jnp.concatenate([a[h * rows:(h + 1) * rows, :] for h in range(HEADS)], axis=1)


def _gla_kernel(q_ref, k_ref, la_ref, v_ref, gg_ref, s0_ref, g_ref, og_ref, sout_ref, s_scr, b_scr, v_scr):
    t = pl.program_id(1)
    rows = q_ref.shape[1]
    nchunk = rows // CHUNK
    C = CHUNK

    @pl.when(t == 0)
    def _():
        s_scr[...] = s0_ref[0]

    r64 = lax.broadcasted_iota(jnp.int32, (C, C), 0)
    c64 = lax.broadcasted_iota(jnp.int32, (C, C), 1)
    tril = jnp.where(c64 <= r64, 1.0, 0.0).astype(BF16)
    rq = lax.broadcasted_iota(jnp.int32, (HEADS * C, GLA_QK), 0)
    cq = lax.broadcasted_iota(jnp.int32, (HEADS * C, GLA_QK), 1)
    head_lanes = (rq >> CHUNK_SHIFT) == (cq >> DK_SHIFT)
    ra = lax.broadcasted_iota(jnp.int32, (HEADS * C, C), 0)
    ca = lax.broadcasted_iota(jnp.int32, (HEADS * C, C), 1)
    causal = ca <= (ra & (C - 1))
    row_i = lax.broadcasted_iota(jnp.int32, (C, GLA_QK), 0)
    re = lax.broadcasted_iota(jnp.int32, (GLA_QK, GLA_V), 0)
    ce = lax.broadcasted_iota(jnp.int32, (GLA_QK, GLA_V), 1)
    expand = jnp.where((re >> DK_SHIFT) == (ce >> DV_SHIFT), 1.0, 0.0).astype(BF16)

    chunk_decay = jnp.sum(la_ref[0].reshape(nchunk, C, GLA_QK), axis=1)
    fast = jnp.min(chunk_decay) >= -GLA_FAST_DECAY_LIMIT

    def intra_fast(base, q, k, v, b, qstack):
        kt = (k * jnp.exp(-b)).astype(BF16)
        a = lax.dot_general(qstack, kt, (((1,), (1,)), ((), ())), preferred_element_type=F32)
        a = jnp.where(causal, a, 0.0).astype(BF16)
        return jnp.concatenate(
            [jnp.dot(a[h * C:(h + 1) * C, :], v[:, h * GLA_DV:(h + 1) * GLA_DV],
                     preferred_element_type=F32) for h in range(HEADS)], axis=0)

    def intra_exact(base, q, k, v, b, qstack):
        b_scr[...] = b
        v_scr[...] = v.astype(F32)

        def key_row(j, acc):
            kj = k_ref[0, pl.ds(base + j, 1), :]
            bj = b_scr[pl.ds(j, 1), :]
            vj = v_scr[pl.ds(j, 1), :]
            w = jnp.where(row_i >= j, jnp.exp(jnp.minimum(b - bj, 0.0)), 0.0) * q * kj
            return acc + jnp.dot(w.astype(BF16), expand, preferred_element_type=F32) * vj

        acc = lax.fori_loop(0, C, key_row, jnp.zeros((C, GLA_V), F32))
        return _stack_heads(acc, GLA_DV)

    def chunk(base, s_old, intra):
        q = q_ref[0, pl.ds(base, C), :]
        k = k_ref[0, pl.ds(base, C), :]
        la = la_ref[0, pl.ds(base, C), :]
        v = v_ref[0, pl.ds(base, C), :]
        la1 = la.astype(BF16)
        r1 = la - la1.astype(F32)
        la2 = r1.astype(BF16)
        la3 = (r1 - la2.astype(F32)).astype(BF16)
        b = (jnp.dot(tril, la1, preferred_element_type=F32)
             + jnp.dot(tril, la2, preferred_element_type=F32)
             + jnp.dot(tril, la3, preferred_element_type=F32))
        b_last = b[C - 1:C, :]
        qt = (q * jnp.exp(b)).astype(BF16)
        ks = (k * jnp.exp(b_last - b)).astype(BF16)
        qstack = jnp.where(head_lanes, jnp.concatenate([qt] * HEADS, axis=0), jnp.zeros((), BF16))
        o = jnp.dot(qstack, s_old.astype(BF16), preferred_element_type=F32)
        o = o + intra(base, q, k, v, b, qstack)
        og = o * lax.rsqrt(jnp.mean(o * o, axis=-1, keepdims=True) + EPS) * g_ref[...]
        gate = _stack_heads(gg_ref[0, pl.ds(base, C), :], GLA_DV)
        og = og * (gate * _sigmoid(gate))
        og_ref[0, pl.ds(base, C), :] = _unstack_heads(og, C).astype(og_ref.dtype)
        ds_full = lax.dot_general(ks, v, (((0,), (0,)), ((), ())), preferred_element_type=F32)
        ds = jnp.concatenate(
            [ds_full[h * GLA_DK:(h + 1) * GLA_DK, h * GLA_DV:(h + 1) * GLA_DV] for h in range(HEADS)], axis=0)
        decay = jnp.transpose(jnp.broadcast_to(jnp.exp(b_last), (GLA_DV, GLA_QK)))
        return decay * s_old + ds

    def fast_step(s):
        for c in range(nchunk):
            s = chunk(c * C, s, intra_fast)
        return s

    def exact_step(s):
        return lax.fori_loop(0, nchunk, lambda c, s: chunk(pl.multiple_of(c * C, C), s, intra_exact), s)

    s_scr[...] = lax.cond(fast, fast_step, exact_step, s_scr[...])

    @pl.when(t == pl.num_programs(1) - 1)
    def _():
        sout_ref[0] = s_scr[...]


def _gla(qg, kg, la, vg, gg, s0, g_stack):
    batch, seq, _ = qg.shape
    rows = min(seq, 512)
    assert seq % rows == 0 and rows % CHUNK == 0
    grid = (batch, seq // rows)
    row = lambda w: pl.BlockSpec((1, rows, w), lambda b, t: (b, t, 0))
    state = pl.BlockSpec((1, GLA_QK, GLA_DV), lambda b, t: (b, 0, 0))
    return pl.pallas_call(
        _gla_kernel,
        grid=grid,
        in_specs=[row(GLA_QK), row(GLA_QK), row(GLA_QK), row(GLA_V), row(GLA_V), state,
                  pl.BlockSpec((GLA_QK, GLA_DV), lambda b, t: (0, 0))],
        out_specs=[row(GLA_V), state],
        out_shape=[jax.ShapeDtypeStruct((batch, seq, GLA_V), BF16),
                   jax.ShapeDtypeStruct((batch, GLA_QK, GLA_DV), F32)],
        scratch_shapes=[pltpu.VMEM((GLA_QK, GLA_DV), F32),
                        pltpu.VMEM((CHUNK, GLA_QK), F32),
                        pltpu.VMEM((CHUNK, GLA_V), F32)],
        compiler_params=_cparams("arbitrary", "arbitrary"),
        name="gla_chunks",
    )(qg, kg, la, vg, gg, s0, g_stack)


def _split_maps(qh):
    lane = lax.broadcasted_iota(jnp.int32, qh.shape, 1)
    zero = jnp.zeros_like(qh)
    return jnp.concatenate([jnp.where(lane < DIFF_DK, qh, zero), jnp.where(lane >= DIFF_DK, qh, zero)], axis=0)


def _flash_update(q2, kb, vb, bias, m_ref, l_ref, acc_ref):
    s = lax.dot_general(q2, kb, (((1,), (1,)), ((), ())), preferred_element_type=F32)
    if bias is not None:
        s = s + bias
    m_prev = m_ref[...]
    m_next = jnp.maximum(m_prev, jnp.max(s, axis=1, keepdims=True))
    p = jnp.exp(s - m_next[:, :1])
    alpha = jnp.exp(m_prev - m_next)
    l_ref[...] = alpha * l_ref[...] + jnp.sum(p, axis=1, keepdims=True)
    acc_ref[...] = alpha * acc_ref[...] + jnp.dot(p.astype(BF16), vb, preferred_element_type=F32)
    m_ref[...] = m_next


def _flash_init(m_ref, l_ref, acc_ref):
    m_ref[...] = jnp.full(m_ref.shape, -jnp.inf, F32)
    l_ref[...] = jnp.zeros(l_ref.shape, F32)
    acc_ref[...] = jnp.zeros(acc_ref.shape, F32)


def _diff_lambda(lam_ref, lam_init):
    lv = lam_ref[...]
    s1 = jnp.sum(lv[0:1, :] * lv[1:2, :], axis=1, keepdims=True)
    s2 = jnp.sum(lv[2:3, :] * lv[3:4, :], axis=1, keepdims=True)
    return jnp.exp(s1) - jnp.exp(s2) + lam_init


def _diff_finish(tq, lam, lam_init, g, l_ref, acc_ref):
    out = acc_ref[...] / l_ref[...]
    od = out[:tq, :] - lam * out[tq:, :]
    od = od * lax.rsqrt(jnp.mean(od * od, axis=-1, keepdims=True) + EPS) * g
    return od * (1.0 - lam_init)


def _flash_update_t(q2, kb, vtb, bias_t, m_ref, l_ref, acc_ref):
    s = lax.dot_general(kb, q2, (((1,), (1,)), ((), ())), preferred_element_type=F32)
    if bias_t is not None:
        s = s + bias_t
    m_prev = m_ref[...]
    m_next = jnp.maximum(m_prev, jnp.max(s, axis=0, keepdims=True))
    p = jnp.exp(s - m_next)
    alpha = jnp.exp(m_prev - m_next)
    l_ref[...] = alpha * l_ref[...] + jnp.sum(p, axis=0, keepdims=True)
    acc_ref[...] = alpha * acc_ref[...] + jnp.dot(vtb, p.astype(BF16), preferred_element_type=F32)
    m_ref[...] = m_next


def _attn_prompt_kernel(lam_init, q_ref, k_ref, vt_ref, bias_ref, lam_ref, g_ref, o_ref, m_ref, l_ref, acc_ref):
    qi = pl.program_id(1)
    tq = q_ref.shape[1]
    lam = _diff_lambda(lam_ref, lam_init)
    _flash_init(m_ref, l_ref, acc_ref)
    cols = [slice(h * 2 * DIFF_DK, (h + 1) * 2 * DIFF_DK) for h in range(HEADS)]
    vrows = [slice(h * DIFF_DV, (h + 1) * DIFF_DV) for h in range(HEADS)]
    q2 = [_split_maps(q_ref[0, :, cols[h]]) for h in range(HEADS)]

    def block(kb, tile):
        base = pl.multiple_of(kb * ATT_BLOCK, ATT_BLOCK)
        for h in range(HEADS):
            bias_t = None if tile is None else jnp.concatenate([bias_ref[tile, h]] * 2, axis=1)
            _flash_update_t(q2[h], k_ref[0, pl.ds(base, ATT_BLOCK), cols[h]],
                            vt_ref[0, vrows[h], pl.ds(base, ATT_BLOCK)], bias_t,
                            m_ref.at[h], l_ref.at[h], acc_ref.at[h])

    def far_block(kb, carry):
        block(kb, None)
        return carry

    lax.fori_loop(0, jnp.maximum(qi - 1, 0), far_block, 0)

    @pl.when(qi >= 1)
    def _():
        block(qi - 1, 1)

    block(qi, 0)
    for h in range(HEADS):
        out = acc_ref[h] * (1.0 / l_ref[h])
        od = out[:, :tq] - lam * out[:, tq:]
        od = od * lax.rsqrt(jnp.mean(od * od, axis=0, keepdims=True) + EPS)
        od = jnp.transpose(od) * g_ref[:, vrows[h]] * (1.0 - lam_init)
        o_ref[0, :, vrows[h]] = od.astype(o_ref.dtype)


def _attn_prompt(qd, kdb, vtb, bias_t, lam_vecs, g_row, lam_init):
    batch, seq, _ = qd.shape
    tq = ATT_BLOCK
    assert seq % tq == 0
    return pl.pallas_call(
        functools.partial(_attn_prompt_kernel, lam_init),
        grid=(batch, seq // tq),
        in_specs=[
            pl.BlockSpec((1, tq, DIFF_QK), lambda b, i: (b, i, 0)),
            pl.BlockSpec((1, seq, DIFF_QK), lambda b, i: (b, 0, 0)),
            pl.BlockSpec((1, DIFF_V, seq), lambda b, i: (b, 0, 0)),
            pl.BlockSpec((2, HEADS, ATT_BLOCK, ATT_BLOCK), lambda b, i: (0, 0, 0, 0)),
            pl.BlockSpec((4, DIFF_DK), lambda b, i: (0, 0)),
            pl.BlockSpec((1, DIFF_V), lambda b, i: (0, 0)),
        ],
        out_specs=pl.BlockSpec((1, tq, DIFF_V), lambda b, i: (b, i, 0)),
        out_shape=jax.ShapeDtypeStruct((batch, seq, DIFF_V), BF16),
        scratch_shapes=[pltpu.VMEM((HEADS, 1, 2 * tq), F32), pltpu.VMEM((HEADS, 1, 2 * tq), F32),
                        pltpu.VMEM((HEADS, DIFF_DV, 2 * tq), F32)],
        compiler_params=_cparams("arbitrary", "arbitrary"),
        name="diff_attention_prompt",
    )(qd, kdb, vtb, bias_t, lam_vecs, g_row)


def _attn_sample_kernel(lam_init, q_ref, ck_ref, cv_ref, nk_ref, nv_ref, bias_ref, lam_ref, g_ref, o_ref,
                        m_ref, l_ref, acc_ref):
    step = pl.program_id(1)
    nstep = pl.num_programs(1)
    tq = q_ref.shape[1]
    tk = ck_ref.shape[2]
    nsub = tk // ATT_BLOCK

    @pl.when(step == 0)
    def _():
        _flash_init(m_ref, l_ref, acc_ref)

    def head_refs(h):
        return m_ref.at[h], l_ref.at[h], acc_ref.at[h]

    def cache_blocks(with_bias):
        for h in range(HEADS):
            cols = slice(h * 2 * DIFF_DK, (h + 1) * 2 * DIFF_DK)
            vcols = slice(h * DIFF_DV, (h + 1) * DIFF_DV)
            q2 = _split_maps(q_ref[0, :, cols])
            for sb in range(nsub):
                rows = slice(sb * ATT_BLOCK, (sb + 1) * ATT_BLOCK)
                bias = None
                if with_bias and sb == nsub - 1:
                    bias = jnp.concatenate([bias_ref[1, h, :tq, :]] * 2, axis=0)
                _flash_update(q2, ck_ref[0, 0, rows, h, :].astype(BF16), cv_ref[0, 0, rows, h, :].astype(BF16),
                              bias, *head_refs(h))

    @pl.when(step < nstep - 1)
    def _():
        cache_blocks(False)

    @pl.when(step == nstep - 1)
    def _():
        cache_blocks(True)
        lam = _diff_lambda(lam_ref, lam_init)
        for h in range(HEADS):
            cols = slice(h * 2 * DIFF_DK, (h + 1) * 2 * DIFF_DK)
            vcols = slice(h * DIFF_DV, (h + 1) * DIFF_DV)
            q2 = _split_maps(q_ref[0, :, cols])
            bias = jnp.concatenate([bias_ref[0, h, :tq, :tq]] * 2, axis=0)
            mh, lh, ah = head_refs(h)
            _flash_update(q2, nk_ref[0, :, cols], nv_ref[0, :, vcols], bias, mh, lh, ah)
            o_ref[0, :, vcols] = _diff_finish(tq, lam, lam_init, g_ref[:, vcols], lh, ah).astype(o_ref.dtype)


def _attn_sample(qd, cache_k, cache_v, layer, kdb, vdb, bias, lam_vecs, g_row, lam_init):
    batch, tq, _ = qd.shape
    past = cache_k.shape[2]
    assert tq == CHUNK and past % CHUNK == 0, "sample queries must be one chunk, appended at a chunk boundary"
    tk = min(past, 1024)
    assert past % tk == 0 and tk % ATT_BLOCK == 0
    new = lambda w: pl.BlockSpec((1, tq, w), lambda b, s: (b, 0, 0))
    blk = pl.BlockSpec((1, 1, tk, HEADS, DIFF_DV), lambda b, s: (layer, b, s, 0, 0))
    return pl.pallas_call(
        functools.partial(_attn_sample_kernel, lam_init),
        grid=(batch, past // tk),
        in_specs=[
            new(DIFF_QK), blk, blk, new(DIFF_QK), new(DIFF_V),
            pl.BlockSpec((2, HEADS, ATT_BLOCK, ATT_BLOCK), lambda b, s: (0, 0, 0, 0)),
            pl.BlockSpec((4, DIFF_DK), lambda b, s: (0, 0)),
            pl.BlockSpec((1, DIFF_V), lambda b, s: (0, 0)),
        ],
        out_specs=new(DIFF_V),
        out_shape=jax.ShapeDtypeStruct((batch, tq, DIFF_V), BF16),
        scratch_shapes=[pltpu.VMEM((HEADS, 2 * tq, DIFF_DV), F32)] * 3,
        compiler_params=_cparams("arbitrary", "arbitrary"),
        name="diff_attention_sample",
    )(qd, cache_k, cache_v, kdb, vdb, bias, lam_vecs, g_row)


def _out_mlp_kernel(final, x_ref, og_ref, od_ref, mod_ref, n2_ref, fg_ref, wo_ref, wu_ref, wd_ref, o_ref):
    nb, tt, _ = x_ref.shape
    rows = nb * tt
    mod = mod_ref[...]
    o = jnp.concatenate([og_ref[...].reshape(rows, GLA_V), od_ref[...].reshape(rows, DIFF_V)], axis=1)
    attn = jnp.dot(o, wo_ref[...], preferred_element_type=F32).reshape(nb, tt, D_MODEL)
    x1 = x_ref[...] + mod[:, 2:3, :] * attn
    ms = jnp.mean(x1 * x1, axis=-1, keepdims=True)
    h2 = x1 * lax.rsqrt(ms + EPS) * n2_ref[...]
    h2 = (h2 * (1.0 + mod[:, 4:5, :]) + mod[:, 3:4, :]).reshape(rows, D_MODEL).astype(BF16)
    ff = jnp.zeros((rows, D_MODEL), F32)
    for c in range(D_FF // D_MODEL):
        cs = slice(c * D_MODEL, (c + 1) * D_MODEL)
        u = jnp.maximum(jnp.dot(h2, wu_ref[:, cs], preferred_element_type=F32), 0.0)
        ff = ff + jnp.dot((u * u).astype(BF16), wd_ref[cs, :], preferred_element_type=F32)
    x2 = x1 + mod[:, 5:6, :] * ff.reshape(nb, tt, D_MODEL)
    if final:
        ms2 = jnp.mean(x2 * x2, axis=-1, keepdims=True)
        x2 = x2 * lax.rsqrt(ms2 + EPS) * fg_ref[...]
    o_ref[...] = x2


def _out_mlp(x, og, od, mod, n2, final_g, w_out, w_up, w_down, final):
    batch, seq, _ = x.shape
    nb, tt = _row_tiling(batch, seq, 512)
    row = lambda w: pl.BlockSpec((nb, tt, w), lambda b, t: (b, t, 0))
    const = lambda shape: pl.BlockSpec(shape, lambda b, t: (0,) * len(shape), pipeline_mode=pl.Buffered(1))
    return pl.pallas_call(
        functools.partial(_out_mlp_kernel, final),
        grid=(batch // nb, seq // tt),
        in_specs=[
            row(D_MODEL), row(GLA_V), row(DIFF_V),
            pl.BlockSpec((nb, 6, D_MODEL), lambda b, t: (b, 0, 0)),
            const((1, 1, D_MODEL)), const((1, 1, D_MODEL)),
            const((D_MODEL, D_MODEL)), const((D_MODEL, D_FF)), const((D_FF, D_MODEL)),
        ],
        out_specs=row(D_MODEL),
        out_shape=jax.ShapeDtypeStruct((batch, seq, D_MODEL), F32),
        compiler_params=_cparams("arbitrary", "arbitrary"),
        name="out_proj_mlp",
    )(x, og, od, mod, n2.reshape(1, 1, D_MODEL), final_g.reshape(1, 1, D_MODEL), w_out, w_up, w_down)


def _pack_w_in(w_in_l):
    a = 2 * GLA_QK + 2 * GLA_V
    main = jnp.concatenate([w_in_l[:, :a], w_in_l[:, a + GLA_RANK:]], axis=1)
    ar = jnp.pad(w_in_l[:, a:a + GLA_RANK], ((0, 0), (0, RANK_PAD - GLA_RANK)))
    return jnp.concatenate([main, ar], axis=1).astype(BF16)


def _trunk(x, mod_all, state0, cache_k, cache_v, params, bias):
    batch, seq, _ = x.shape
    prompt = cache_k is None
    new_k = new_v = None
    ss = []
    for l in range(DEPTH):
        p = params[l]
        lam_init = 0.8 - 0.6 * math.exp(-0.3 * l)
        mod = mod_all[l]
        qg, kg, vg, gg, la, qd, new_k, new_v, kdb, vdb = _in_proj(
            x, mod, p["n1"], p["w_all"], p["wa2p"], p["ba"], l, new_k, new_v, transposed_v=prompt)
        if state0 is None:
            s0 = jnp.zeros((batch, GLA_QK, GLA_DV), F32)
        else:
            s0 = state0[l].reshape(batch, GLA_QK, GLA_DV)
        og, s_new = _gla(qg, kg, la, vg, gg, s0, p["g_stack"])
        if prompt:
            od = _attn_prompt(qd, kdb, vdb, bias[1], p["lam_vecs"], p["diff_g"], lam_init)
        else:
            od = _attn_sample(qd, cache_k, cache_v, l, kdb, vdb, bias[0], p["lam_vecs"], p["diff_g"], lam_init)
        x = _out_mlp(x, og, od, mod, p["n2"], p["final_g"], p["w_out"], p["w_up"], p["w_down"], l == DEPTH - 1)
        ss.append(s_new.reshape(batch, HEADS, GLA_DK, GLA_DV))
    return x, new_k, new_v, jnp.stack(ss)


def kernel(x_prompt, x_sample, cache_k, cache_v, state_gla, c_prompt, c_sample, w_mod, b_mod, norm1_g, w_in, w_a2, b_a, gla_norm_g, lam_q1, lam_k1, lam_q2, lam_k2, diff_norm_g, rel_bias, w_out, norm2_g, w_up, w_down, final_g):
    nbp = x_prompt.shape[0]
    mod_all = _modulation(jnp.concatenate([c_prompt, c_sample], axis=0), w_mod, b_mod)
    mod_all = mod_all.reshape(DEPTH, -1, 6, D_MODEL)
    bias = _bias_tiles(rel_bias)
    params = []
    for l in range(DEPTH):
        params.append(dict(
            n1=norm1_g[l], n2=norm2_g[l], final_g=final_g,
            w_all=_pack_w_in(w_in[l]),
            wa2p=jnp.pad(w_a2[l], ((0, RANK_PAD - GLA_RANK), (0, 0))).astype(BF16),
            ba=b_a[l],
            g_stack=jnp.repeat(gla_norm_g[l], CHUNK, axis=0),
            lam_vecs=jnp.stack([lam_q1[l], lam_k1[l], lam_q2[l], lam_k2[l]]),
            diff_g=diff_norm_g[l].reshape(1, DIFF_V),
            w_out=w_out[l].astype(BF16), w_up=w_up[l].astype(BF16), w_down=w_down[l].astype(BF16),
        ))
    y_p, k_p, v_p, s_p = _trunk(x_prompt, mod_all[:, :nbp], None, None, None, params, bias)
    y_s, k_s, v_s, s_s = _trunk(x_sample, mod_all[:, nbp:], state_gla, cache_k, cache_v, params, bias)
    return (y_p, y_s, k_p, v_p, s_p, k_s, v_s, s_s)
```

```python
import functools
import math

import jax
import jax.numpy as jnp
import numpy as np
from jax import lax
from jax.experimental import pallas as pl
from jax.experimental.pallas import tpu as pltpu

F32 = jnp.float32
BF16 = jnp.bfloat16

D_MODEL = 1024
DEPTH = 4
CHUNK = 64
HEADS = 4
GLA_DK = 64
GLA_DV = 128
GLA_RANK = 16
GLA_TAU = 16.0
DIFF_DK = 64
DIFF_DV = 128
N_BUCKETS = 32
D_FF = 4 * D_MODEL
EPS = 1e-6

GLA_QK = HEADS * GLA_DK
GLA_V = HEADS * GLA_DV
DIFF_QK = HEADS * 2 * DIFF_DK
DIFF_V = HEADS * DIFF_DV
RANK_PAD = 128
PROJ_W = 2 * GLA_QK + 2 * GLA_V + 2 * DIFF_QK + DIFF_V + RANK_PAD

V7X_VMEM_LIMIT_BYTES = 60000 * 1024
ATT_BLOCK = 256
ATT_PAIR = 2 * ATT_BLOCK
FAR_BUCKET = 15
LOG2E = math.log2(math.e)
ATT_VROWS = DIFF_DV + 16
GLA_FAST_DECAY_LIMIT = 30.0
CHUNK_SHIFT = 6
DK_SHIFT = 6
DV_SHIFT = 7


def _cparams(*sem):
    return pltpu.CompilerParams(dimension_semantics=sem, vmem_limit_bytes=V7X_VMEM_LIMIT_BYTES)


def _sigmoid(x):
    return 1.0 / (1.0 + jnp.exp(-x))


def _mod_kernel(c_ref, w_ref, b_ref, o_ref):
    c = c_ref[...]
    ca = (c * _sigmoid(c)).astype(BF16)
    o_ref[0] = jnp.dot(ca, w_ref[0].astype(BF16), preferred_element_type=F32) + b_ref[0]


def _modulation(c_all, w_mod, b_mod):
    nb = c_all.shape[0]
    ncol = 6 * D_MODEL // D_MODEL
    return pl.pallas_call(
        _mod_kernel,
        grid=(DEPTH, ncol),
        in_specs=[
            pl.BlockSpec((nb, D_MODEL), lambda l, j: (0, 0)),
            pl.BlockSpec((1, D_MODEL, D_MODEL), lambda l, j: (l, 0, j)),
            pl.BlockSpec((1, 1, D_MODEL), lambda l, j: (l, 0, j)),
        ],
        out_specs=pl.BlockSpec((1, nb, D_MODEL), lambda l, j: (l, 0, j)),
        out_shape=jax.ShapeDtypeStruct((DEPTH, nb, 6 * D_MODEL), F32),
        compiler_params=_cparams("arbitrary", "arbitrary"),
        name="adaln_modulation",
    )(c_all, w_mod, b_mod.reshape(DEPTH, 1, 6 * D_MODEL))


_BUCKET_THRESHOLDS = (12, 16, 23, 32, 46, 64, 91)


def _bias_kernel(rb_ref, o_ref):
    orient = pl.program_id(0)
    t = pl.program_id(1)
    h = pl.program_id(2)
    r = lax.broadcasted_iota(jnp.int32, (ATT_BLOCK, ATT_BLOCK), 0)
    c = lax.broadcasted_iota(jnp.int32, (ATT_BLOCK, ATT_BLOCK), 1)
    i = r + orient * (c - r)
    j = c + orient * (r - c)
    rel = j - i - t * ATT_BLOCK
    n = jnp.abs(rel)
    large = jnp.full_like(n, 8)
    for thr in _BUCKET_THRESHOLDS:
        large = large + jnp.where(n >= thr, 1, 0)
    bucket = jnp.where(rel > 0, N_BUCKETS // 2, 0) + jnp.where(n < 8, n, large)
    far = rb_ref[FAR_BUCKET, h]
    val = jnp.zeros((ATT_BLOCK, ATT_BLOCK), F32)
    for b in range(N_BUCKETS):
        val = jnp.where(bucket == b, (rb_ref[b, h] - far) * LOG2E, val)
    visible = (j >> CHUNK_SHIFT) <= (i >> CHUNK_SHIFT) + t * ATT_BLOCK
    o_ref[0, 0, 0] = jnp.where(visible, val, -jnp.inf)


def _bias_tiles(rel_bias):
    return pl.pallas_call(
        _bias_kernel,
        grid=(2, 2, HEADS),
        in_specs=[pl.BlockSpec(memory_space=pltpu.SMEM)],
        out_specs=pl.BlockSpec((1, 1, 1, ATT_BLOCK, ATT_BLOCK), lambda o, t, h: (o, t, h, 0, 0)),
        out_shape=jax.ShapeDtypeStruct((2, 2, HEADS, ATT_BLOCK, ATT_BLOCK), F32),
        compiler_params=_cparams("arbitrary", "arbitrary", "arbitrary"),
        name="t5_bias_tiles",
    )(rel_bias)


def _in_proj_kernel(n_alias, transposed_v, x_ref, mod_ref, n1_ref, w_ref, wa2_ref, ba_ref, *refs):
    qg_ref, kg_ref, vg_ref, gg_ref, la_ref, qd_ref, kd_ref, vd_ref, kdb_ref, vdb_ref = refs[n_alias:]
    nb, tt, _ = x_ref.shape
    x = x_ref[...]
    ms = jnp.mean(x * x, axis=-1, keepdims=True)
    y = x * lax.rsqrt(ms + EPS) * n1_ref[...]
    mod = mod_ref[...]
    h = y * (1.0 + mod[:, 1:2, :]) + mod[:, 0:1, :]
    hb = h.reshape(nb * tt, D_MODEL).astype(BF16)
    proj = jnp.dot(hb, w_ref[...], preferred_element_type=F32)

    def put(ref, lo, width, scale=None):
        v = proj[:, lo:lo + width]
        if scale is not None:
            v = v * scale
        ref[...] = v.reshape(nb, tt, width).astype(ref.dtype)
        return lo + width

    lo = put(qg_ref, 0, GLA_QK, GLA_DK ** -0.5)
    lo = put(kg_ref, lo, GLA_QK)
    lo = put(vg_ref, lo, GLA_V)
    lo = put(gg_ref, lo, GLA_V)
    lo = put(qd_ref, lo, DIFF_QK, DIFF_DK ** -0.5 * LOG2E)

    def put_cache(ref, lo):
        for h in range(HEADS):
            slab = proj[:, lo + h * DIFF_DV:lo + (h + 1) * DIFF_DV]
            for b in range(nb):
                ref[0, b, pl.ds(h, tt, stride=HEADS), :] = slab[b * tt:(b + 1) * tt, :]

    put_cache(kd_ref, lo)
    lo = put(kdb_ref, lo, DIFF_QK)
    put_cache(vd_ref, lo)
    if transposed_v:
        pad_row = lax.broadcasted_iota(jnp.int32, (ATT_VROWS - DIFF_DV, nb * tt), 0)
        ones_row = jnp.where(pad_row == 0, 1.0, 0.0).astype(vdb_ref.dtype)
        for h in range(HEADS):
            vt = jnp.transpose(proj[:, lo + h * DIFF_DV:lo + (h + 1) * DIFF_DV]).astype(vdb_ref.dtype)
            vdb_ref[0, h * ATT_VROWS:h * ATT_VROWS + DIFF_DV, :] = vt
            vdb_ref[0, h * ATT_VROWS + DIFF_DV:(h + 1) * ATT_VROWS, :] = ones_row
        lo = lo + DIFF_V
    else:
        lo = put(vdb_ref, lo, DIFF_V)
    ar = proj[:, lo:lo + RANK_PAD].astype(BF16)
    z = jnp.dot(ar, wa2_ref[...], preferred_element_type=F32) + ba_ref[...]
    la = (jnp.minimum(z, 0.0) - jnp.log(1.0 + jnp.exp(-jnp.abs(z)))) * (1.0 / GLA_TAU)
    la_ref[...] = la.reshape(nb, tt, GLA_QK)


def _row_tiling(batch, seq, target):
    if seq >= target:
        assert seq % target == 0
        return 1, target
    nb = min(batch, max(1, target // seq))
    while batch % nb:
        nb -= 1
    return nb, seq


def _in_proj(x, mod, n1, w_all, wa2p, ba, layer, new_k, new_v, transposed_v):
    batch, seq, _ = x.shape
    nb, tt = (1, min(seq, 512)) if transposed_v else _row_tiling(batch, seq, 512)
    assert seq % tt == 0
    grid = (batch // nb, seq // tt)
    row = lambda w: pl.BlockSpec((nb, tt, w), lambda b, t: (b, t, 0))
    const = lambda shape: pl.BlockSpec(shape, lambda b, t: (0,) * len(shape))
    cache = pl.BlockSpec((1, nb, tt * HEADS, DIFF_DV), lambda b, t: (layer, b, t, 0))
    cache_shape = jax.ShapeDtypeStruct((DEPTH, batch, seq * HEADS, DIFF_DV), F32)
    sds = lambda w, dt: jax.ShapeDtypeStruct((batch, seq, w), dt)
    if transposed_v:
        vb_spec = pl.BlockSpec((1, HEADS * ATT_VROWS, tt), lambda b, t: (b, 0, t))
        vb_shape = jax.ShapeDtypeStruct((batch, HEADS * ATT_VROWS, seq), BF16)
    else:
        vb_spec, vb_shape = row(DIFF_V), sds(DIFF_V, BF16)
    out_specs = [row(GLA_QK), row(GLA_QK), row(GLA_V), row(GLA_V), row(GLA_QK), row(DIFF_QK),
                 cache, cache, row(DIFF_QK), vb_spec]
    out_shape = [sds(GLA_QK, F32), sds(GLA_QK, F32), sds(GLA_V, BF16), sds(GLA_V, F32), sds(GLA_QK, F32),
                 sds(DIFF_QK, BF16), cache_shape, cache_shape, sds(DIFF_QK, BF16), vb_shape]
    in_specs = [
        row(D_MODEL),
        pl.BlockSpec((nb, 6, D_MODEL), lambda b, t: (b, 0, 0)),
        const((1, 1, D_MODEL)),
        const((D_MODEL, PROJ_W)),
        const((RANK_PAD, GLA_QK)),
        const((1, GLA_QK)),
    ]
    args = [x, mod, n1.reshape(1, 1, D_MODEL), w_all, wa2p, ba.reshape(1, GLA_QK)]
    aliases = {}
    if new_k is not None:
        aliases = {len(args): 6, len(args) + 1: 7}
        in_specs += [pl.BlockSpec(memory_space=pl.ANY)] * 2
        args += [new_k, new_v]
    return pl.pallas_call(
        functools.partial(_in_proj_kernel, len(aliases), transposed_v),
        grid=grid,
        in_specs=in_specs,
        out_specs=out_specs,
        out_shape=out_shape,
        input_output_aliases=aliases,
        compiler_params=_cparams("arbitrary", "arbitrary"),
        name="in_projection",
    )(*args)


def _stack_heads(a, width):
    return jnp.concatenate([a[:, h * width:(h + 1) * width] for h in range(HEADS)], axis=0)


def _unstack_heads(a, rows):
    return jnp.concatenate([a[h * rows:(h + 1) * rows, :] for h in range(HEADS)], axis=1)


def _gla_kernel(q_ref, k_ref, la_ref, v_ref, gg_ref, s0_ref, g_ref, og_ref, sout_ref, s_scr, b_scr, v_scr):
    t = pl.program_id(1)
    rows = q_ref.shape[1]
    nchunk = rows // CHUNK
    C = CHUNK

    @pl.when(t == 0)
    def _():
        s_scr[...] = s0_ref[0]

    r64 = lax.broadcasted_iota(jnp.int32, (C, C), 0)
    c64 = lax.broadcasted_iota(jnp.int32, (C, C), 1)
    tril = jnp.where(c64 <= r64, 1.0, 0.0).astype(BF16)
    rq = lax.broadcasted_iota(jnp.int32, (HEADS * C, GLA_QK), 0)
    cq = lax.broadcasted_iota(jnp.int32, (HEADS * C, GLA_QK), 1)
    head_lanes = (rq >> CHUNK_SHIFT) == (cq >> DK_SHIFT)
    ra = lax.broadcasted_iota(jnp.int32, (HEADS * C, C), 0)
    ca = lax.broadcasted_iota(jnp.int32, (HEADS * C, C), 1)
    causal = ca <= (ra & (C - 1))
    row_i = lax.broadcasted_iota(jnp.int32, (C, GLA_QK), 0)
    re = lax.broadcasted_iota(jnp.int32, (GLA_QK, GLA_V), 0)
    ce = lax.broadcasted_iota(jnp.int32, (GLA_QK, GLA_V), 1)
    expand = jnp.where((re >> DK_SHIFT) == (ce >> DV_SHIFT), 1.0, 0.0).astype(BF16)

    chunk_decay = jnp.sum(la_ref[0].reshape(nchunk, C, GLA_QK), axis=1)
    fast = jnp.min(chunk_decay) >= -GLA_FAST_DECAY_LIMIT

    def intra_fast(base, q, k, v, b, qstack):
        kt = (k * jnp.exp(-b)).astype(BF16)
        a = lax.dot_general(qstack, kt, (((1,), (1,)), ((), ())), preferred_element_type=F32)
        a = jnp.where(causal, a, 0.0).astype(BF16)
        return jnp.concatenate(
            [jnp.dot(a[h * C:(h + 1) * C, :], v[:, h * GLA_DV:(h + 1) * GLA_DV],
                     preferred_element_type=F32) for h in range(HEADS)], axis=0)

    def intra_exact(base, q, k, v, b, qstack):
        b_scr[...] = b
        v_scr[...] = v.astype(F32)

        def key_row(j, acc):
            kj = k_ref[0, pl.ds(base + j, 1), :]
            bj = b_scr[pl.ds(j, 1), :]
            vj = v_scr[pl.ds(j, 1), :]
            w = jnp.where(row_i >= j, jnp.exp(jnp.minimum(b - bj, 0.0)), 0.0) * q * kj
            return acc + jnp.dot(w.astype(BF16), expand, preferred_element_type=F32) * vj

        acc = lax.fori_loop(0, C, key_row, jnp.zeros((C, GLA_V), F32))
        return _stack_heads(acc, GLA_DV)

    def chunk(base, s_old, intra):
        q = q_ref[0, pl.ds(base, C), :]
        k = k_ref[0, pl.ds(base, C), :]
        la = la_ref[0, pl.ds(base, C), :]
        v = v_ref[0, pl.ds(base, C), :]
        la1 = la.astype(BF16)
        r1 = la - la1.astype(F32)
        la2 = r1.astype(BF16)
        la3 = (r1 - la2.astype(F32)).astype(BF16)
        b = (jnp.dot(tril, la1, preferred_element_type=F32)
             + jnp.dot(tril, la2, preferred_element_type=F32)
             + jnp.dot(tril, la3, preferred_element_type=F32))
        b_last = b[C - 1:C, :]
        qt = (q * jnp.exp(b)).astype(BF16)
        ks = (k * jnp.exp(b_last - b)).astype(BF16)
        qstack = jnp.where(head_lanes, jnp.concatenate([qt] * HEADS, axis=0), jnp.zeros((), BF16))
        o = jnp.dot(qstack, s_old.astype(BF16), preferred_element_type=F32)
        o = o + intra(base, q, k, v, b, qstack)
        og = o * lax.rsqrt(jnp.mean(o * o, axis=-1, keepdims=True) + EPS) * g_ref[...]
        gate = _stack_heads(gg_ref[0, pl.ds(base, C), :], GLA_DV)
        og = og * (gate * _sigmoid(gate))
        og_ref[0, pl.ds(base, C), :] = _unstack_heads(og, C).astype(og_ref.dtype)
        ds_full = lax.dot_general(ks, v, (((0,), (0,)), ((), ())), preferred_element_type=F32)
        ds = jnp.concatenate(
            [ds_full[h * GLA_DK:(h + 1) * GLA_DK, h * GLA_DV:(h + 1) * GLA_DV] for h in range(HEADS)], axis=0)
        decay = jnp.transpose(jnp.broadcast_to(jnp.exp(b_last), (GLA_DV, GLA_QK)))
        return decay * s_old + ds

    def fast_step(s):
        for c in range(nchunk):
            s = chunk(c * C, s, intra_fast)
        return s

    def exact_step(s):
        return lax.fori_loop(0, nchunk, lambda c, s: chunk(pl.multiple_of(c * C, C), s, intra_exact), s)

    s_scr[...] = lax.cond(fast, fast_step, exact_step, s_scr[...])

    @pl.when(t == pl.num_programs(1) - 1)
    def _():
        sout_ref[0] = s_scr[...]


def _gla(qg, kg, la, vg, gg, s0, g_stack):
    batch, seq, _ = qg.shape
    rows = min(seq, 512)
    assert seq % rows == 0 and rows % CHUNK == 0
    grid = (batch, seq // rows)
    row = lambda w: pl.BlockSpec((1, rows, w), lambda b, t: (b, t, 0))
    state = pl.BlockSpec((1, GLA_QK, GLA_DV), lambda b, t: (b, 0, 0))
    return pl.pallas_call(
        _gla_kernel,
        grid=grid,
        in_specs=[row(GLA_QK), row(GLA_QK), row(GLA_QK), row(GLA_V), row(GLA_V), state,
                  pl.BlockSpec((GLA_QK, GLA_DV), lambda b, t: (0, 0))],
        out_specs=[row(GLA_V), state],
        out_shape=[jax.ShapeDtypeStruct((batch, seq, GLA_V), BF16),
                   jax.ShapeDtypeStruct((batch, GLA_QK, GLA_DV), F32)],
        scratch_shapes=[pltpu.VMEM((GLA_QK, GLA_DV), F32),
                        pltpu.VMEM((CHUNK, GLA_QK), F32),
                        pltpu.VMEM((CHUNK, GLA_V), F32)],
        compiler_params=_cparams("arbitrary", "arbitrary"),
        name="gla_chunks",
    )(qg, kg, la, vg, gg, s0, g_stack)


def _split_maps(qh):
    lane = lax.broadcasted_iota(jnp.int32, qh.shape, 1)
    zero = jnp.zeros_like(qh)
    return jnp.concatenate([jnp.where(lane < DIFF_DK, qh, zero), jnp.where(lane >= DIFF_DK, qh, zero)], axis=0)


def _diff_lambda(lam_ref, lam_init):
    lv = lam_ref[...]
    s1 = jnp.sum(lv[0:1, :] * lv[1:2, :], axis=1, keepdims=True)
    s2 = jnp.sum(lv[2:3, :] * lv[3:4, :], axis=1, keepdims=True)
    return jnp.exp(s1) - jnp.exp(s2) + lam_init


_NT = (((1,), (1,)), ((), ()))


def _prompt_scores(k_ref, q2, base, size, dst):
    for h in range(HEADS):
        cols = slice(h * 2 * DIFF_DK, (h + 1) * 2 * DIFF_DK)
        dst[h][0:size, :] = lax.dot_general(k_ref[0, pl.ds(base, size), cols], q2[h], _NT,
                                            preferred_element_type=F32)


def _prompt_softmax_step(src, row0, size, vt_ref, base, bias_t_ref, tile, m_ref, acc_ref):
    for h in range(HEADS):
        s = src[h][pl.ds(row0, size), :]
        if tile is not None:
            s = s + jnp.concatenate([bias_t_ref[tile, h]] * 2, axis=1)
        m_prev = m_ref[h]
        m_next = jnp.maximum(m_prev, jnp.max(s, axis=0, keepdims=True))
        p = jnp.exp2(s - m_next)
        alpha = jnp.exp2(m_prev - m_next)
        vtb = vt_ref[0, h * ATT_VROWS:(h + 1) * ATT_VROWS, pl.ds(base, size)]
        acc_ref[h] = alpha * acc_ref[h] + jnp.dot(vtb, p.astype(BF16), preferred_element_type=F32)
        m_ref[h] = m_next


def _attn_prompt_kernel(lam_init, q_ref, k_ref, vt_ref, bias_ref, lam_ref, g_ref, o_ref, m_ref, acc_ref, *scr):
    far_scr, diag_scr = scr[:HEADS], scr[HEADS:]
    qi = pl.program_id(1)
    tq = q_ref.shape[1]
    lam = _diff_lambda(lam_ref, lam_init)
    m_ref[...] = jnp.full(m_ref.shape, -jnp.inf, F32)
    acc_ref[...] = jnp.zeros(acc_ref.shape, F32)
    q2 = [_split_maps(q_ref[0, :, h * 2 * DIFF_DK:(h + 1) * 2 * DIFF_DK]) for h in range(HEADS)]
    nfar = jnp.maximum(qi - 1, 0)
    npair = nfar // 2
    odd = nfar - 2 * npair

    _prompt_scores(k_ref, q2, pl.multiple_of(qi * ATT_BLOCK, ATT_BLOCK), ATT_BLOCK, diag_scr)

    def far_pair(p, carry):
        base = pl.multiple_of(p * ATT_PAIR, ATT_PAIR)
        _prompt_scores(k_ref, q2, base, ATT_PAIR, far_scr)
        _prompt_softmax_step(far_scr, 0, ATT_PAIR, vt_ref, base, bias_ref, None, m_ref, acc_ref)
        return carry

    lax.fori_loop(0, npair, far_pair, 0)
    tail = pl.multiple_of(npair * ATT_PAIR, ATT_BLOCK)
    _prompt_scores(k_ref, q2, tail, ATT_PAIR, far_scr)

    @pl.when(odd == 1)
    def _():
        _prompt_softmax_step(far_scr, 0, ATT_BLOCK, vt_ref, tail, bias_ref, None, m_ref, acc_ref)

    @pl.when(qi >= 1)
    def _():
        _prompt_softmax_step(far_scr, pl.multiple_of(odd * ATT_BLOCK, ATT_BLOCK), ATT_BLOCK, vt_ref,
                             pl.multiple_of((qi - 1) * ATT_BLOCK, ATT_BLOCK), bias_ref, 1, m_ref, acc_ref)

    _prompt_softmax_step(diag_scr, 0, ATT_BLOCK, vt_ref, pl.multiple_of(qi * ATT_BLOCK, ATT_BLOCK),
                         bias_ref, 0, m_ref, acc_ref)
    for h in range(HEADS):
        a = acc_ref[h]
        out = a[:DIFF_DV, :] * (1.0 / a[DIFF_DV:DIFF_DV + 1, :])
        od = out[:, :tq] - lam * out[:, tq:]
        od = od * lax.rsqrt(jnp.mean(od * od, axis=0, keepdims=True) + EPS)
        vcols = slice(h * DIFF_DV, (h + 1) * DIFF_DV)
        od = jnp.transpose(od) * g_ref[:, vcols] * (1.0 - lam_init)
        o_ref[0, :, vcols] = od.astype(o_ref.dtype)


def _attn_prompt(qd, kdb, vtb, bias_t, lam_vecs, g_row, lam_init):
    batch, seq, _ = qd.shape
    tq = ATT_BLOCK
    assert seq % tq == 0 and seq >= ATT_PAIR
    return pl.pallas_call(
        functools.partial(_attn_prompt_kernel, lam_init),
        grid=(batch, seq // tq),
        in_specs=[
            pl.BlockSpec((1, tq, DIFF_QK), lambda b, i: (b, i, 0)),
            pl.BlockSpec((1, seq, DIFF_QK), lambda b, i: (b, 0, 0)),
            pl.BlockSpec((1, HEADS * ATT_VROWS, seq), lambda b, i: (b, 0, 0)),
            pl.BlockSpec((2, HEADS, ATT_BLOCK, ATT_BLOCK), lambda b, i: (0, 0, 0, 0)),
            pl.BlockSpec((4, DIFF_DK), lambda b, i: (0, 0)),
            pl.BlockSpec((1, DIFF_V), lambda b, i: (0, 0)),
        ],
        out_specs=pl.BlockSpec((1, tq, DIFF_V), lambda b, i: (b, i, 0)),
        out_shape=jax.ShapeDtypeStruct((batch, seq, DIFF_V), BF16),
        scratch_shapes=[pltpu.VMEM((HEADS, 1, 2 * tq), F32), pltpu.VMEM((HEADS, ATT_VROWS, 2 * tq), F32)]
                       + [pltpu.VMEM((ATT_PAIR, 2 * tq), F32)] * HEADS
                       + [pltpu.VMEM((ATT_BLOCK, 2 * tq), F32)] * HEADS,
        compiler_params=_cparams("arbitrary", "arbitrary"),
        name="diff_attention_prompt",
    )(qd, kdb, vtb, bias_t, lam_vecs, g_row)


def _with_ones_column(v):
    lane = lax.broadcasted_iota(jnp.int32, v.shape, 1)
    return jnp.concatenate([v, jnp.where(lane == 0, 1.0, 0.0).astype(v.dtype)], axis=1)


def _sample_softmax_step(s, v_aug, m_ref, acc_ref, h):
    m_prev = m_ref[h]
    m_next = jnp.maximum(m_prev, jnp.max(s, axis=1, keepdims=True))
    p = jnp.exp2(s - m_next[:, :1])
    alpha = jnp.exp2(m_prev - m_next)
    acc_ref[h] = (jnp.concatenate([alpha, alpha], axis=1) * acc_ref[h]
                  + jnp.dot(p.astype(BF16), v_aug, preferred_element_type=F32))
    m_ref[h] = m_next


def _attn_sample_kernel(lam_init, q_ref, ck_ref, cv_ref, nk_ref, nv_ref, bias_ref, lam_ref, g_ref, o_ref,
                        m_ref, acc_ref, *s_scr):
    step = pl.program_id(1)
    nstep = pl.num_programs(1)
    tq = q_ref.shape[1]
    tk = ck_ref.shape[2] // HEADS

    def head_rows(ref, h):
        return ref[0, 0, pl.ds(h, tk, stride=HEADS), :].astype(BF16)

    @pl.when(step == 0)
    def _():
        m_ref[...] = jnp.full(m_ref.shape, -jnp.inf, F32)
        acc_ref[...] = jnp.zeros(acc_ref.shape, F32)

    q2 = [_split_maps(q_ref[0, :, h * 2 * DIFF_DK:(h + 1) * 2 * DIFF_DK]) for h in range(HEADS)]
    for h in range(HEADS):
        s_scr[h][...] = lax.dot_general(q2[h], head_rows(ck_ref, h), _NT, preferred_element_type=F32)

    def cache_block(near):
        for h in range(HEADS):
            s = s_scr[h][...]
            if near:
                bias = jnp.concatenate([bias_ref[1, h, :tq, :]] * 2, axis=0)
                s = jnp.concatenate([s[:, :tk - ATT_BLOCK], s[:, tk - ATT_BLOCK:] + bias], axis=1)
            _sample_softmax_step(s, _with_ones_column(head_rows(cv_ref, h)), m_ref, acc_ref, h)

    @pl.when(step < nstep - 1)
    def _():
        cache_block(False)

    @pl.when(step == nstep - 1)
    def _():
        cache_block(True)
        lam = _diff_lambda(lam_ref, lam_init)
        for h in range(HEADS):
            cols = slice(h * 2 * DIFF_DK, (h + 1) * 2 * DIFF_DK)
            vcols = slice(h * DIFF_DV, (h + 1) * DIFF_DV)
            s = lax.dot_general(q2[h], nk_ref[0, :, cols], _NT, preferred_element_type=F32)
            s = s + jnp.concatenate([bias_ref[0, h, :tq, :tq]] * 2, axis=0)
            _sample_softmax_step(s, _with_ones_column(nv_ref[0, :, vcols]), m_ref, acc_ref, h)
            a = acc_ref[h]
            out = a[:, :DIFF_DV] * (1.0 / a[:, DIFF_DV:DIFF_DV + 1])
            od = out[:tq, :] - lam * out[tq:, :]
            od = od * lax.rsqrt(jnp.mean(od * od, axis=-1, keepdims=True) + EPS) * g_ref[:, vcols]
            o_ref[0, :, vcols] = (od * (1.0 - lam_init)).astype(o_ref.dtype)


def _attn_sample(qd, cache_k, cache_v, layer, kdb, vdb, bias, lam_vecs, g_row, lam_init):
    batch, tq, _ = qd.shape
    past = cache_k.shape[2] // HEADS
    assert tq == CHUNK and past % CHUNK == 0, "sample queries must be one chunk, appended at a chunk boundary"
    tk = min(past, 1024)
    assert past % tk == 0 and tk % ATT_BLOCK == 0
    new = lambda w: pl.BlockSpec((1, tq, w), lambda b, s: (b, 0, 0))
    blk = pl.BlockSpec((1, 1, tk * HEADS, DIFF_DV), lambda b, s: (layer, b, s, 0))
    return pl.pallas_call(
        functools.partial(_attn_sample_kernel, lam_init),
        grid=(batch, past // tk),
        in_specs=[
            new(DIFF_QK), blk, blk, new(DIFF_QK), new(DIFF_V),
            pl.BlockSpec((2, HEADS, ATT_BLOCK, ATT_BLOCK), lambda b, s: (0, 0, 0, 0)),
            pl.BlockSpec((4, DIFF_DK), lambda b, s: (0, 0)),
            pl.BlockSpec((1, DIFF_V), lambda b, s: (0, 0)),
        ],
        out_specs=new(DIFF_V),
        out_shape=jax.ShapeDtypeStruct((batch, tq, DIFF_V), BF16),
        scratch_shapes=[pltpu.VMEM((HEADS, 2 * tq, DIFF_DV), F32), pltpu.VMEM((HEADS, 2 * tq, 2 * DIFF_DV), F32)]
                       + [pltpu.VMEM((2 * tq, tk), F32)] * HEADS,
        compiler_params=_cparams("arbitrary", "arbitrary"),
        name="diff_attention_sample",
    )(qd, cache_k, cache_v, kdb, vdb, bias, lam_vecs, g_row)


def _out_mlp_kernel(final, x_ref, og_ref, od_ref, mod_ref, n2_ref, fg_ref, wo_ref, wu_ref, wd_ref, o_ref):
    nb, tt, _ = x_ref.shape
    rows = nb * tt
    mod = mod_ref[...]
    o = jnp.concatenate([og_ref[...].reshape(rows, GLA_V), od_ref[...].reshape(rows, DIFF_V)], axis=1)
    attn = jnp.dot(o, wo_ref[...], preferred_element_type=F32).reshape(nb, tt, D_MODEL)
    x1 = x_ref[...] + mod[:, 2:3, :] * attn
    ms = jnp.mean(x1 * x1, axis=-1, keepdims=True)
    h2 = x1 * lax.rsqrt(ms + EPS) * n2_ref[...]
    h2 = (h2 * (1.0 + mod[:, 4:5, :]) + mod[:, 3:4, :]).reshape(rows, D_MODEL).astype(BF16)
    ff = jnp.zeros((rows, D_MODEL), F32)
    for c in range(D_FF // D_MODEL):
        cs = slice(c * D_MODEL, (c + 1) * D_MODEL)
        u = jnp.maximum(jnp.dot(h2, wu_ref[:, cs], preferred_element_type=F32), 0.0)
        ff = ff + jnp.dot((u * u).astype(BF16), wd_ref[cs, :], preferred_element_type=F32)
    x2 = x1 + mod[:, 5:6, :] * ff.reshape(nb, tt, D_MODEL)
    if final:
        ms2 = jnp.mean(x2 * x2, axis=-1, keepdims=True)
        x2 = x2 * lax.rsqrt(ms2 + EPS) * fg_ref[...]
    o_ref[...] = x2


def _out_mlp(x, og, od, mod, n2, final_g, w_out, w_up, w_down, final):
    batch, seq, _ = x.shape
    nb, tt = _row_tiling(batch, seq, 512)
    row = lambda w: pl.BlockSpec((nb, tt, w), lambda b, t: (b, t, 0))
    const = lambda shape: pl.BlockSpec(shape, lambda b, t: (0,) * len(shape), pipeline_mode=pl.Buffered(1))
    return pl.pallas_call(
        functools.partial(_out_mlp_kernel, final),
        grid=(batch // nb, seq // tt),
        in_specs=[
            row(D_MODEL), row(GLA_V), row(DIFF_V),
            pl.BlockSpec((nb, 6, D_MODEL), lambda b, t: (b, 0, 0)),
            const((1, 1, D_MODEL)), const((1, 1, D_MODEL)),
            const((D_MODEL, D_MODEL)), const((D_MODEL, D_FF)), const((D_FF, D_MODEL)),
        ],
        out_specs=row(D_MODEL),
        out_shape=jax.ShapeDtypeStruct((batch, seq, D_MODEL), F32),
        compiler_params=_cparams("arbitrary", "arbitrary"),
        name="out_proj_mlp",
    )(x, og, od, mod, n2.reshape(1, 1, D_MODEL), final_g.reshape(1, 1, D_MODEL), w_out, w_up, w_down)


def _pack_w_in(w_in_l):
    a = 2 * GLA_QK + 2 * GLA_V
    main = jnp.concatenate([w_in_l[:, :a], w_in_l[:, a + GLA_RANK:]], axis=1)
    ar = jnp.pad(w_in_l[:, a:a + GLA_RANK], ((0, 0), (0, RANK_PAD - GLA_RANK)))
    return jnp.concatenate([main, ar], axis=1).astype(BF16)


def _trunk(x, mod_all, state0, cache_k, cache_v, params, bias):
    batch, seq, _ = x.shape
    prompt = cache_k is None
    if not prompt:
        cache_k = cache_k.reshape(DEPTH, batch, -1, DIFF_DV)
        cache_v = cache_v.reshape(DEPTH, batch, -1, DIFF_DV)
    new_k = new_v = None
    ss = []
    for l in range(DEPTH):
        p = params[l]
        lam_init = 0.8 - 0.6 * math.exp(-0.3 * l)
        mod = mod_all[l]
        qg, kg, vg, gg, la, qd, new_k, new_v, kdb, vdb = _in_proj(
            x, mod, p["n1"], p["w_all"], p["wa2p"], p["ba"], l, new_k, new_v, transposed_v=prompt)
        if state0 is None:
            s0 = jnp.zeros((batch, GLA_QK, GLA_DV), F32)
        else:
            s0 = state0[l].reshape(batch, GLA_QK, GLA_DV)
        og, s_new = _gla(qg, kg, la, vg, gg, s0, p["g_stack"])
        if prompt:
            od = _attn_prompt(qd, kdb, vdb, bias[1], p["lam_vecs"], p["diff_g"], lam_init)
        else:
            od = _attn_sample(qd, cache_k, cache_v, l, kdb, vdb, bias[0], p["lam_vecs"], p["diff_g"], lam_init)
        x = _out_mlp(x, og, od, mod, p["n2"], p["final_g"], p["w_out"], p["w_up"], p["w_down"], l == DEPTH - 1)
        ss.append(s_new.reshape(batch, HEADS, GLA_DK, GLA_DV))
    cache_shape = (DEPTH, batch, seq, HEADS, DIFF_DV)
    return x, new_k.reshape(cache_shape), new_v.reshape(cache_shape), jnp.stack(ss)


def kernel(x_prompt, x_sample, cache_k, cache_v, state_gla, c_prompt, c_sample, w_mod, b_mod, norm1_g, w_in, w_a2, b_a, gla_norm_g, lam_q1, lam_k1, lam_q2, lam_k2, diff_norm_g, rel_bias, w_out, norm2_g, w_up, w_down, final_g):
    nbp = x_prompt.shape[0]
    mod_all = _modulation(jnp.concatenate([c_prompt, c_sample], axis=0), w_mod, b_mod)
    mod_all = mod_all.reshape(DEPTH, -1, 6, D_MODEL)
    bias = _bias_tiles(rel_bias)
    params = []
    for l in range(DEPTH):
        params.append(dict(
            n1=norm1_g[l], n2=norm2_g[l], final_g=final_g,
            w_all=_pack_w_in(w_in[l]),
            wa2p=jnp.pad(w_a2[l], ((0, RANK_PAD - GLA_RANK), (0, 0))).astype(BF16),
            ba=b_a[l],
            g_stack=jnp.repeat(gla_norm_g[l], CHUNK, axis=0),
            lam_vecs=jnp.stack([lam_q1[l], lam_k1[l], lam_q2[l], lam_k2[l]]),
            diff_g=diff_norm_g[l].reshape(1, DIFF_V),
            w_out=w_out[l].astype(BF16), w_up=w_up[l].astype(BF16), w_down=w_down[l].astype(BF16),
        ))
    y_p, k_p, v_p, s_p = _trunk(x_prompt, mod_all[:, :nbp], None, None, None, params, bias)
    y_s, k_s, v_s, s_s = _trunk(x_sample, mod_all[:, nbp:], state_gla, cache_k, cache_v, params, bias)
    return (y_p, y_s, k_p, v_p, s_p, k_s, v_s, s_s)
```

```python
import functools
import math

import jax
import jax.numpy as jnp
import numpy as np
from jax import lax
from jax.experimental import pallas as pl
from jax.experimental.pallas import tpu as pltpu

F32 = jnp.float32
BF16 = jnp.bfloat16

D_MODEL = 1024
DEPTH = 4
CHUNK = 64
HEADS = 4
GLA_DK = 64
GLA_DV = 128
GLA_RANK = 16
GLA_TAU = 16.0
DIFF_DK = 64
DIFF_DV = 128
N_BUCKETS = 32
D_FF = 4 * D_MODEL
EPS = 1e-6

GLA_QK = HEADS * GLA_DK
GLA_V = HEADS * GLA_DV
DIFF_QK = HEADS * 2 * DIFF_DK
DIFF_V = HEADS * DIFF_DV
RANK_PAD = 128
PROJ_W = 2 * GLA_QK + 2 * GLA_V + 2 * DIFF_QK + DIFF_V + RANK_PAD

V7X_VMEM_LIMIT_BYTES = 60000 * 1024
ATT_BLOCK = 256
ATT_PAIR = 2 * ATT_BLOCK
SAMPLE_KEY_BLOCK = 2048
FAR_BUCKET = 15
LOG2E = math.log2(math.e)
ATT_VROWS = DIFF_DV + 16
GLA_FAST_DECAY_LIMIT = 30.0
GLA_FAST_SPAN = 256
CHUNK_SHIFT = 6
DK_SHIFT = 6
DV_SHIFT = 7


def _cparams(*sem):
    return pltpu.CompilerParams(dimension_semantics=sem, vmem_limit_bytes=V7X_VMEM_LIMIT_BYTES)


def _sigmoid(x):
    return 1.0 / (1.0 + jnp.exp(-x))


def _mod_kernel(c_ref, w_ref, b_ref, o_ref):
    c = c_ref[...]
    ca = (c * _sigmoid(c)).astype(BF16)
    o_ref[0] = jnp.dot(ca, w_ref[0].astype(BF16), preferred_element_type=F32) + b_ref[0]


def _modulation(c_all, w_mod, b_mod):
    nb = c_all.shape[0]
    ncol = 6 * D_MODEL // D_MODEL
    return pl.pallas_call(
        _mod_kernel,
        grid=(DEPTH, ncol),
        in_specs=[
            pl.BlockSpec((nb, D_MODEL), lambda l, j: (0, 0)),
            pl.BlockSpec((1, D_MODEL, D_MODEL), lambda l, j: (l, 0, j)),
            pl.BlockSpec((1, 1, D_MODEL), lambda l, j: (l, 0, j)),
        ],
        out_specs=pl.BlockSpec((1, nb, D_MODEL), lambda l, j: (l, 0, j)),
        out_shape=jax.ShapeDtypeStruct((DEPTH, nb, 6 * D_MODEL), F32),
        compiler_params=_cparams("arbitrary", "arbitrary"),
        name="adaln_modulation",
    )(c_all, w_mod, b_mod.reshape(DEPTH, 1, 6 * D_MODEL))


_BUCKET_THRESHOLDS = (12, 16, 23, 32, 46, 64, 91)


def _bias_kernel(rb_ref, o_ref):
    orient = pl.program_id(0)
    t = pl.program_id(1)
    h = pl.program_id(2)
    r = lax.broadcasted_iota(jnp.int32, (ATT_BLOCK, ATT_BLOCK), 0)
    c = lax.broadcasted_iota(jnp.int32, (ATT_BLOCK, ATT_BLOCK), 1)
    i = r + orient * (c - r)
    j = c + orient * (r - c)
    rel = j - i - t * ATT_BLOCK
    n = jnp.abs(rel)
    large = jnp.full_like(n, 8)
    for thr in _BUCKET_THRESHOLDS:
        large = large + jnp.where(n >= thr, 1, 0)
    bucket = jnp.where(rel > 0, N_BUCKETS // 2, 0) + jnp.where(n < 8, n, large)
    far = rb_ref[FAR_BUCKET, h]
    val = jnp.zeros((ATT_BLOCK, ATT_BLOCK), F32)
    for b in range(N_BUCKETS):
        val = jnp.where(bucket == b, (rb_ref[b, h] - far) * LOG2E, val)
    visible = (j >> CHUNK_SHIFT) <= (i >> CHUNK_SHIFT) + t * ATT_BLOCK
    o_ref[0, 0, 0] = jnp.where(visible, val, -jnp.inf)


def _bias_tiles(rel_bias):
    return pl.pallas_call(
        _bias_kernel,
        grid=(2, 2, HEADS),
        in_specs=[pl.BlockSpec(memory_space=pltpu.SMEM)],
        out_specs=pl.BlockSpec((1, 1, 1, ATT_BLOCK, ATT_BLOCK), lambda o, t, h: (o, t, h, 0, 0)),
        out_shape=jax.ShapeDtypeStruct((2, 2, HEADS, ATT_BLOCK, ATT_BLOCK), F32),
        compiler_params=_cparams("arbitrary", "arbitrary", "arbitrary"),
        name="t5_bias_tiles",
    )(rel_bias)


def _in_proj_kernel(n_alias, transposed_v, x_ref, mod_ref, n1_ref, w_ref, wa2_ref, ba_ref, *refs):
    qg_ref, kg_ref, vg_ref, gg_ref, la_ref, qd_ref, kd_ref, vd_ref, kdb_ref, vdb_ref = refs[n_alias:]
    nb, tt, _ = x_ref.shape
    x = x_ref[...]
    ms = jnp.mean(x * x, axis=-1, keepdims=True)
    y = x * lax.rsqrt(ms + EPS) * n1_ref[...]
    mod = mod_ref[...]
    h = y * (1.0 + mod[:, 1:2, :]) + mod[:, 0:1, :]
    hb = h.reshape(nb * tt, D_MODEL).astype(BF16)
    proj = jnp.dot(hb, w_ref[...], preferred_element_type=F32)

    def put(ref, lo, width, scale=None):
        v = proj[:, lo:lo + width]
        if scale is not None:
            v = v * scale
        ref[...] = v.reshape(nb, tt, width).astype(ref.dtype)
        return lo + width

    lo = put(qg_ref, 0, GLA_QK, GLA_DK ** -0.5)
    lo = put(kg_ref, lo, GLA_QK)
    lo = put(vg_ref, lo, GLA_V)
    lo = put(gg_ref, lo, GLA_V)
    lo = put(qd_ref, lo, DIFF_QK, DIFF_DK ** -0.5 * LOG2E)

    def put_cache(ref, lo):
        for h in range(HEADS):
            slab = proj[:, lo + h * DIFF_DV:lo + (h + 1) * DIFF_DV]
            for b in range(nb):
                ref[0, b, pl.ds(h, tt, stride=HEADS), :] = slab[b * tt:(b + 1) * tt, :]

    put_cache(kd_ref, lo)
    lo = put(kdb_ref, lo, DIFF_QK)
    put_cache(vd_ref, lo)
    if transposed_v:
        pad_row = lax.broadcasted_iota(jnp.int32, (ATT_VROWS - DIFF_DV, nb * tt), 0)
        ones_row = jnp.where(pad_row == 0, 1.0, 0.0).astype(vdb_ref.dtype)
        for h in range(HEADS):
            vt = jnp.transpose(proj[:, lo + h * DIFF_DV:lo + (h + 1) * DIFF_DV]).astype(vdb_ref.dtype)
            vdb_ref[0, h * ATT_VROWS:h * ATT_VROWS + DIFF_DV, :] = vt
            vdb_ref[0, h * ATT_VROWS + DIFF_DV:(h + 1) * ATT_VROWS, :] = ones_row
        lo = lo + DIFF_V
    else:
        lo = put(vdb_ref, lo, DIFF_V)
    ar = proj[:, lo:lo + RANK_PAD].astype(BF16)
    z = jnp.dot(ar, wa2_ref[...], preferred_element_type=F32) + ba_ref[...]
    la = (jnp.minimum(z, 0.0) - jnp.log(1.0 + jnp.exp(-jnp.abs(z)))) * (1.0 / GLA_TAU)
    la_ref[...] = la.reshape(nb, tt, GLA_QK)


def _row_tiling(batch, seq, target):
    if seq >= target:
        assert seq % target == 0
        return 1, target
    nb = min(batch, max(1, target // seq))
    while batch % nb:
        nb -= 1
    return nb, seq


def _in_proj(x, mod, n1, w_all, wa2p, ba, layer, new_k, new_v, transposed_v):
    batch, seq, _ = x.shape
    nb, tt = (1, min(seq, 512)) if transposed_v else _row_tiling(batch, seq, 512)
    assert seq % tt == 0
    grid = (batch // nb, seq // tt)
    row = lambda w: pl.BlockSpec((nb, tt, w), lambda b, t: (b, t, 0))
    const = lambda shape: pl.BlockSpec(shape, lambda b, t: (0,) * len(shape))
    cache = pl.BlockSpec((1, nb, tt * HEADS, DIFF_DV), lambda b, t: (layer, b, t, 0))
    cache_shape = jax.ShapeDtypeStruct((DEPTH, batch, seq * HEADS, DIFF_DV), F32)
    sds = lambda w, dt: jax.ShapeDtypeStruct((batch, seq, w), dt)
    if transposed_v:
        vb_spec = pl.BlockSpec((1, HEADS * ATT_VROWS, tt), lambda b, t: (b, 0, t))
        vb_shape = jax.ShapeDtypeStruct((batch, HEADS * ATT_VROWS, seq), BF16)
    else:
        vb_spec, vb_shape = row(DIFF_V), sds(DIFF_V, BF16)
    out_specs = [row(GLA_QK), row(GLA_QK), row(GLA_V), row(GLA_V), row(GLA_QK), row(DIFF_QK),
                 cache, cache, row(DIFF_QK), vb_spec]
    out_shape = [sds(GLA_QK, F32), sds(GLA_QK, F32), sds(GLA_V, BF16), sds(GLA_V, F32), sds(GLA_QK, F32),
                 sds(DIFF_QK, BF16), cache_shape, cache_shape, sds(DIFF_QK, BF16), vb_shape]
    in_specs = [
        row(D_MODEL),
        pl.BlockSpec((nb, 6, D_MODEL), lambda b, t: (b, 0, 0)),
        const((1, 1, D_MODEL)),
        const((D_MODEL, PROJ_W)),
        const((RANK_PAD, GLA_QK)),
        const((1, GLA_QK)),
    ]
    args = [x, mod, n1.reshape(1, 1, D_MODEL), w_all, wa2p, ba.reshape(1, GLA_QK)]
    aliases = {}
    if new_k is not None:
        aliases = {len(args): 6, len(args) + 1: 7}
        in_specs += [pl.BlockSpec(memory_space=pl.ANY)] * 2
        args += [new_k, new_v]
    return pl.pallas_call(
        functools.partial(_in_proj_kernel, len(aliases), transposed_v),
        grid=grid,
        in_specs=in_specs,
        out_specs=out_specs,
        out_shape=out_shape,
        input_output_aliases=aliases,
        compiler_params=_cparams("arbitrary", "arbitrary"),
        name="in_projection",
    )(*args)


def _stack_heads(a, width):
    return jnp.concatenate([a[:, h * width:(h + 1) * width] for h in range(HEADS)], axis=0)


def _unstack_heads(a, rows):
    return jnp.concatenate([a[h * rows:(h + 1) * rows, :] for h in range(HEADS)], axis=1)


def _gla_kernel(q_ref, k_ref, la_ref, v_ref, gg_ref, s0_ref, g_ref, og_ref, sout_ref, s_scr, b_scr, v_scr):
    t = pl.program_id(1)
    rows = q_ref.shape[1]
    span = min(rows, GLA_FAST_SPAN)
    assert rows % span == 0 and span & (span - 1) == 0

    @pl.when(t == 0)
    def _():
        s_scr[...] = s0_ref[0]

    def masks(n):
        r = lax.broadcasted_iota(jnp.int32, (n, n), 0)
        c = lax.broadcasted_iota(jnp.int32, (n, n), 1)
        tril = jnp.where(c <= r, 1.0, 0.0).astype(BF16)
        rq = lax.broadcasted_iota(jnp.int32, (HEADS * n, GLA_QK), 0)
        cq = lax.broadcasted_iota(jnp.int32, (HEADS * n, GLA_QK), 1)
        head_lanes = (rq >> (n.bit_length() - 1)) == (cq >> DK_SHIFT)
        ra = lax.broadcasted_iota(jnp.int32, (HEADS * n, n), 0)
        ca = lax.broadcasted_iota(jnp.int32, (HEADS * n, n), 1)
        return tril, head_lanes, ca <= (ra & (n - 1))

    span_decay = jnp.sum(la_ref[0].reshape(rows // span, span, GLA_QK), axis=1)
    fast = jnp.min(span_decay) >= -GLA_FAST_DECAY_LIMIT

    def head_rows(a, h, n):
        return a[h * n:(h + 1) * n, :]

    def intra_fast(n, causal):
        def intra(base, q, k, v, b, qstack):
            kt = (k * jnp.exp(-b)).astype(BF16)
            a = lax.dot_general(qstack, kt, _NT, preferred_element_type=F32)
            a = jnp.where(causal, a, 0.0).astype(BF16)
            return jnp.concatenate(
                [jnp.dot(head_rows(a, h, n), v[:, h * GLA_DV:(h + 1) * GLA_DV], preferred_element_type=F32)
                 for h in range(HEADS)], axis=0)
        return intra

    def intra_exact(n):
        row_i = lax.broadcasted_iota(jnp.int32, (n, GLA_QK), 0)
        re = lax.broadcasted_iota(jnp.int32, (GLA_QK, GLA_V), 0)
        ce = lax.broadcasted_iota(jnp.int32, (GLA_QK, GLA_V), 1)
        expand = jnp.where((re >> DK_SHIFT) == (ce >> DV_SHIFT), 1.0, 0.0).astype(BF16)

        def intra(base, q, k, v, b, qstack):
            b_scr[...] = b
            v_scr[...] = v.astype(F32)

            def key_row(j, acc):
                kj = k_ref[0, pl.ds(base + j, 1), :]
                bj = b_scr[pl.ds(j, 1), :]
                vj = v_scr[pl.ds(j, 1), :]
                w = jnp.where(row_i >= j, jnp.exp(jnp.minimum(b - bj, 0.0)), 0.0) * q * kj
                return acc + jnp.dot(w.astype(BF16), expand, preferred_element_type=F32) * vj

            acc = lax.fori_loop(0, n, key_row, jnp.zeros((n, GLA_V), F32))
            return _stack_heads(acc, GLA_DV)
        return intra

    def process(n, base, s_old, tril, head_lanes, intra):
        q = q_ref[0, pl.ds(base, n), :]
        k = k_ref[0, pl.ds(base, n), :]
        la = la_ref[0, pl.ds(base, n), :]
        v = v_ref[0, pl.ds(base, n), :]
        la1 = la.astype(BF16)
        la2 = (la - la1.astype(F32)).astype(BF16)
        b = jnp.dot(tril, la1, preferred_element_type=F32) + jnp.dot(tril, la2, preferred_element_type=F32)
        b_last = b[n - 1:n, :]
        qt = (q * jnp.exp(b)).astype(BF16)
        kst = jnp.transpose(k * jnp.exp(b_last - b)).astype(BF16)
        qstack = jnp.where(head_lanes, jnp.concatenate([qt] * HEADS, axis=0), jnp.zeros((), BF16))
        o = jnp.dot(qstack, s_old.astype(BF16), preferred_element_type=F32)
        o = o + intra(base, q, k, v, b, qstack)
        gain = jnp.concatenate([jnp.broadcast_to(g_ref[h:h + 1, :], (n, GLA_DV)) for h in range(HEADS)], axis=0)
        og = o * lax.rsqrt(jnp.mean(o * o, axis=-1, keepdims=True) + EPS) * gain
        gate = _stack_heads(gg_ref[0, pl.ds(base, n), :], GLA_DV)
        og = og * (gate * _sigmoid(gate))
        og_ref[0, pl.ds(base, n), :] = _unstack_heads(og, n).astype(og_ref.dtype)
        ds = jnp.concatenate(
            [jnp.dot(head_rows(kst, h, GLA_DK), v[:, h * GLA_DV:(h + 1) * GLA_DV], preferred_element_type=F32)
             for h in range(HEADS)], axis=0)
        decay = jnp.transpose(jnp.broadcast_to(jnp.exp(b_last), (GLA_DV, GLA_QK)))
        return decay * s_old + ds

    def fast_step(s):
        tril, head_lanes, causal = masks(span)
        for i in range(rows // span):
            s = process(span, i * span, s, tril, head_lanes, intra_fast(span, causal))
        return s

    def exact_step(s):
        tril, head_lanes, _ = masks(CHUNK)
        intra = intra_exact(CHUNK)
        return lax.fori_loop(
            0, rows // CHUNK,
            lambda c, s: process(CHUNK, pl.multiple_of(c * CHUNK, CHUNK), s, tril, head_lanes, intra), s)

    s_scr[...] = lax.cond(fast, fast_step, exact_step, s_scr[...])

    @pl.when(t == pl.num_programs(1) - 1)
    def _():
        sout_ref[0] = s_scr[...]


def _gla(qg, kg, la, vg, gg, s0, gain):
    batch, seq, _ = qg.shape
    rows = min(seq, 512)
    assert seq % rows == 0 and rows % CHUNK == 0
    grid = (batch, seq // rows)
    row = lambda w: pl.BlockSpec((1, rows, w), lambda b, t: (b, t, 0))
    state = pl.BlockSpec((1, GLA_QK, GLA_DV), lambda b, t: (b, 0, 0))
    return pl.pallas_call(
        _gla_kernel,
        grid=grid,
        in_specs=[row(GLA_QK), row(GLA_QK), row(GLA_QK), row(GLA_V), row(GLA_V), state,
                  pl.BlockSpec((HEADS, GLA_DV), lambda b, t: (0, 0))],
        out_specs=[row(GLA_V), state],
        out_shape=[jax.ShapeDtypeStruct((batch, seq, GLA_V), BF16),
                   jax.ShapeDtypeStruct((batch, GLA_QK, GLA_DV), F32)],
        scratch_shapes=[pltpu.VMEM((GLA_QK, GLA_DV), F32),
                        pltpu.VMEM((CHUNK, GLA_QK), F32),
                        pltpu.VMEM((CHUNK, GLA_V), F32)],
        compiler_params=_cparams("arbitrary", "arbitrary"),
        name="gla_chunks",
    )(qg, kg, la, vg, gg, s0, gain)


def _split_maps(qh):
    lane = lax.broadcasted_iota(jnp.int32, qh.shape, 1)
    zero = jnp.zeros_like(qh)
    return jnp.concatenate([jnp.where(lane < DIFF_DK, qh, zero), jnp.where(lane >= DIFF_DK, qh, zero)], axis=0)


def _diff_lambda(lam_ref, lam_init):
    lv = lam_ref[...]
    s1 = jnp.sum(lv[0:1, :] * lv[1:2, :], axis=1, keepdims=True)
    s2 = jnp.sum(lv[2:3, :] * lv[3:4, :], axis=1, keepdims=True)
    return jnp.exp(s1) - jnp.exp(s2) + lam_init


_NT = (((1,), (1,)), ((), ()))


def _prompt_scores(k_ref, q2, base, size, dst):
    for h in range(HEADS):
        cols = slice(h * 2 * DIFF_DK, (h + 1) * 2 * DIFF_DK)
        dst[h][0:size, :] = lax.dot_general(k_ref[0, pl.ds(base, size), cols], q2[h], _NT,
                                            preferred_element_type=F32)


def _prompt_softmax_step(src, size, vt_ref, base, bias_t_ref, tiles, m_ref, acc_ref):
    for h in range(HEADS):
        s = src[h][0:size, :]
        if tiles is not None:
            parts = []
            for i, tile in enumerate(tiles):
                part = s[i * ATT_BLOCK:(i + 1) * ATT_BLOCK, :]
                if tile is not None:
                    part = part + jnp.concatenate([bias_t_ref[tile, h]] * 2, axis=1)
                parts.append(part)
            s = jnp.concatenate(parts, axis=0)
        m_prev = m_ref[h]
        m_next = jnp.maximum(m_prev, jnp.max(s, axis=0, keepdims=True))
        p = jnp.exp2(s - m_next)
        alpha = jnp.exp2(m_prev - m_next)
        vtb = vt_ref[0, h * ATT_VROWS:(h + 1) * ATT_VROWS, pl.ds(base, size)]
        acc_ref[h] = alpha * acc_ref[h] + jnp.dot(vtb, p.astype(BF16), preferred_element_type=F32)
        m_ref[h] = m_next


def _attn_prompt_kernel(lam_init, q_ref, k_ref, vt_ref, bias_ref, lam_ref, g_ref, o_ref, m_ref, acc_ref, *scr):
    s_scr = scr
    qi = pl.program_id(1)
    tq = q_ref.shape[1]
    lam = _diff_lambda(lam_ref, lam_init)
    m_ref[...] = jnp.full(m_ref.shape, -jnp.inf, F32)
    acc_ref[...] = jnp.zeros(acc_ref.shape, F32)
    q2 = [_split_maps(q_ref[0, :, h * 2 * DIFF_DK:(h + 1) * 2 * DIFF_DK]) for h in range(HEADS)]
    nfar = jnp.maximum(qi - 1, 0)
    npair = nfar // 2
    odd = nfar - 2 * npair

    def far_pair(p, carry):
        base = pl.multiple_of(p * ATT_PAIR, ATT_PAIR)
        _prompt_scores(k_ref, q2, base, ATT_PAIR, s_scr)
        _prompt_softmax_step(s_scr, ATT_PAIR, vt_ref, base, bias_ref, None, m_ref, acc_ref)
        return carry

    lax.fori_loop(0, npair, far_pair, 0)
    tail = pl.multiple_of(npair * ATT_PAIR, ATT_BLOCK)

    def tail_step(tiles):
        def run(_):
            size = len(tiles) * ATT_BLOCK
            _prompt_scores(k_ref, q2, tail, size, s_scr)
            _prompt_softmax_step(s_scr, size, vt_ref, tail, bias_ref, tiles, m_ref, acc_ref)
            return 0
        return run

    lax.switch(jnp.where(qi == 0, 0, 1 + odd), [tail_step((0,)), tail_step((1, 0)), tail_step((None, 1, 0))], 0)
    for h in range(HEADS):
        a = acc_ref[h]
        out = a[:DIFF_DV, :] * (1.0 / a[DIFF_DV:DIFF_DV + 1, :])
        od = out[:, :tq] - lam * out[:, tq:]
        od = od * lax.rsqrt(jnp.mean(od * od, axis=0, keepdims=True) + EPS)
        vcols = slice(h * DIFF_DV, (h + 1) * DIFF_DV)
        od = jnp.transpose(od) * g_ref[:, vcols] * (1.0 - lam_init)
        o_ref[0, :, vcols] = od.astype(o_ref.dtype)


def _attn_prompt(qd, kdb, vtb, bias_t, lam_vecs, g_row, lam_init):
    batch, seq, _ = qd.shape
    tq = ATT_BLOCK
    assert seq % tq == 0 and seq >= ATT_PAIR
    return pl.pallas_call(
        functools.partial(_attn_prompt_kernel, lam_init),
        grid=(batch, seq // tq),
        in_specs=[
            pl.BlockSpec((1, tq, DIFF_QK), lambda b, i: (b, i, 0)),
            pl.BlockSpec((1, seq, DIFF_QK), lambda b, i: (b, 0, 0)),
            pl.BlockSpec((1, HEADS * ATT_VROWS, seq), lambda b, i: (b, 0, 0)),
            pl.BlockSpec((2, HEADS, ATT_BLOCK, ATT_BLOCK), lambda b, i: (0, 0, 0, 0)),
            pl.BlockSpec((4, DIFF_DK), lambda b, i: (0, 0)),
            pl.BlockSpec((1, DIFF_V), lambda b, i: (0, 0)),
        ],
        out_specs=pl.BlockSpec((1, tq, DIFF_V), lambda b, i: (b, i, 0)),
        out_shape=jax.ShapeDtypeStruct((batch, seq, DIFF_V), BF16),
        scratch_shapes=[pltpu.VMEM((HEADS, 1, 2 * tq), F32), pltpu.VMEM((HEADS, ATT_VROWS, 2 * tq), F32)]
                       + [pltpu.VMEM((3 * ATT_BLOCK, 2 * tq), F32)] * HEADS,
        compiler_params=_cparams("arbitrary", "arbitrary"),
        name="diff_attention_prompt",
    )(qd, kdb, vtb, bias_t, lam_vecs, g_row)


def _with_ones_column(v):
    lane = lax.broadcasted_iota(jnp.int32, v.shape, 1)
    return jnp.concatenate([v, jnp.where(lane == 0, 1.0, 0.0).astype(v.dtype)], axis=1)


def _sample_softmax_step(s, v_aug, m_ref, acc_ref, h):
    m_prev = m_ref[h]
    m_next = jnp.maximum(m_prev, jnp.max(s, axis=1, keepdims=True))
    p = jnp.exp2(s - m_next[:, :1])
    alpha = jnp.exp2(m_prev - m_next)
    acc_ref[h] = (jnp.concatenate([alpha, alpha], axis=1) * acc_ref[h]
                  + jnp.dot(p.astype(BF16), v_aug, preferred_element_type=F32))
    m_ref[h] = m_next


def _attn_sample_kernel(lam_init, q_ref, ck_ref, cv_ref, nk_ref, nv_ref, bias_ref, lam_ref, g_ref, o_ref,
                        m_ref, acc_ref, *s_scr):
    step = pl.program_id(1)
    nstep = pl.num_programs(1)
    tq = q_ref.shape[1]
    tk = ck_ref.shape[2] // HEADS

    def head_rows(ref, h):
        return ref[0, 0, pl.ds(h, tk, stride=HEADS), :].astype(BF16)

    @pl.when(step == 0)
    def _():
        m_ref[...] = jnp.full(m_ref.shape, -jnp.inf, F32)
        acc_ref[...] = jnp.zeros(acc_ref.shape, F32)

    q2 = [_split_maps(q_ref[0, :, h * 2 * DIFF_DK:(h + 1) * 2 * DIFF_DK]) for h in range(HEADS)]
    for h in range(HEADS):
        s_scr[h][...] = lax.dot_general(q2[h], head_rows(ck_ref, h), _NT, preferred_element_type=F32)

    def cache_block(near):
        for h in range(HEADS):
            s = s_scr[h][...]
            if near:
                bias = jnp.concatenate([bias_ref[1, h, :tq, :]] * 2, axis=0)
                s = jnp.concatenate([s[:, :tk - ATT_BLOCK], s[:, tk - ATT_BLOCK:] + bias], axis=1)
            _sample_softmax_step(s, _with_ones_column(head_rows(cv_ref, h)), m_ref, acc_ref, h)

    @pl.when(step < nstep - 1)
    def _():
        cache_block(False)

    @pl.when(step == nstep - 1)
    def _():
        cache_block(True)
        lam = _diff_lambda(lam_ref, lam_init)
        for h in range(HEADS):
            cols = slice(h * 2 * DIFF_DK, (h + 1) * 2 * DIFF_DK)
            vcols = slice(h * DIFF_DV, (h + 1) * DIFF_DV)
            s = lax.dot_general(q2[h], nk_ref[0, :, cols], _NT, preferred_element_type=F32)
            s = s + jnp.concatenate([bias_ref[0, h, :tq, :tq]] * 2, axis=0)
            _sample_softmax_step(s, _with_ones_column(nv_ref[0, :, vcols]), m_ref, acc_ref, h)
            a = acc_ref[h]
            out = a[:, :DIFF_DV] * (1.0 / a[:, DIFF_DV:DIFF_DV + 1])
            od = out[:tq, :] - lam * out[tq:, :]
            od = od * lax.rsqrt(jnp.mean(od * od, axis=-1, keepdims=True) + EPS) * g_ref[:, vcols]
            o_ref[0, :, vcols] = (od * (1.0 - lam_init)).astype(o_ref.dtype)


def _attn_sample(qd, cache_k, cache_v, layer, kdb, vdb, bias, lam_vecs, g_row, lam_init):
    batch, tq, _ = qd.shape
    past = cache_k.shape[2] // HEADS
    assert tq == CHUNK and past % CHUNK == 0, "sample queries must be one chunk, appended at a chunk boundary"
    tk = min(past, SAMPLE_KEY_BLOCK)
    assert past % tk == 0 and tk % ATT_BLOCK == 0
    new = lambda w: pl.BlockSpec((1, tq, w), lambda b, s: (b, 0, 0))
    blk = pl.BlockSpec((1, 1, tk * HEADS, DIFF_DV), lambda b, s: (layer, b, s, 0))
    return pl.pallas_call(
        functools.partial(_attn_sample_kernel, lam_init),
        grid=(batch, past // tk),
        in_specs=[
            new(DIFF_QK), blk, blk, new(DIFF_QK), new(DIFF_V),
            pl.BlockSpec((2, HEADS, ATT_BLOCK, ATT_BLOCK), lambda b, s: (0, 0, 0, 0)),
            pl.BlockSpec((4, DIFF_DK), lambda b, s: (0, 0)),
            pl.BlockSpec((1, DIFF_V), lambda b, s: (0, 0)),
        ],
        out_specs=new(DIFF_V),
        out_shape=jax.ShapeDtypeStruct((batch, tq, DIFF_V), BF16),
        scratch_shapes=[pltpu.VMEM((HEADS, 2 * tq, DIFF_DV), F32), pltpu.VMEM((HEADS, 2 * tq, 2 * DIFF_DV), F32)]
                       + [pltpu.VMEM((2 * tq, tk), F32)] * HEADS,
        compiler_params=_cparams("arbitrary", "arbitrary"),
        name="diff_attention_sample",
    )(qd, cache_k, cache_v, kdb, vdb, bias, lam_vecs, g_row)


def _out_mlp_kernel(final, x_ref, og_ref, od_ref, mod_ref, n2_ref, fg_ref, wo_ref, wu_ref, wd_ref, o_ref):
    nb, tt, _ = x_ref.shape
    rows = nb * tt
    mod = mod_ref[...]
    o = jnp.concatenate([og_ref[...].reshape(rows, GLA_V), od_ref[...].reshape(rows, DIFF_V)], axis=1)
    attn = jnp.dot(o, wo_ref[...], preferred_element_type=F32).reshape(nb, tt, D_MODEL)
    x1 = x_ref[...] + mod[:, 2:3, :] * attn
    ms = jnp.mean(x1 * x1, axis=-1, keepdims=True)
    h2 = x1 * lax.rsqrt(ms + EPS) * n2_ref[...]
    h2 = (h2 * (1.0 + mod[:, 4:5, :]) + mod[:, 3:4, :]).reshape(rows, D_MODEL).astype(BF16)
    ff = jnp.zeros((rows, D_MODEL), F32)
    for c in range(D_FF // D_MODEL):
        cs = slice(c * D_MODEL, (c + 1) * D_MODEL)
        u = jnp.maximum(jnp.dot(h2, wu_ref[:, cs], preferred_element_type=F32), 0.0)
        ff = ff + jnp.dot((u * u).astype(BF16), wd_ref[cs, :], preferred_element_type=F32)
    x2 = x1 + mod[:, 5:6, :] * ff.reshape(nb, tt, D_MODEL)
    if final:
        ms2 = jnp.mean(x2 * x2, axis=-1, keepdims=True)
        x2 = x2 * lax.rsqrt(ms2 + EPS) * fg_ref[...]
    o_ref[...] = x2


def _out_mlp(x, og, od, mod, n2, final_g, w_out, w_up, w_down, final):
    batch, seq, _ = x.shape
    nb, tt = _row_tiling(batch, seq, 512)
    row = lambda w: pl.BlockSpec((nb, tt, w), lambda b, t: (b, t, 0))
    const = lambda shape: pl.BlockSpec(shape, lambda b, t: (0,) * len(shape), pipeline_mode=pl.Buffered(1))
    return pl.pallas_call(
        functools.partial(_out_mlp_kernel, final),
        grid=(batch // nb, seq // tt),
        in_specs=[
            row(D_MODEL), row(GLA_V), row(DIFF_V),
            pl.BlockSpec((nb, 6, D_MODEL), lambda b, t: (b, 0, 0)),
            const((1, 1, D_MODEL)), const((1, 1, D_MODEL)),
            const((D_MODEL, D_MODEL)), const((D_MODEL, D_FF)), const((D_FF, D_MODEL)),
        ],
        out_specs=row(D_MODEL),
        out_shape=jax.ShapeDtypeStruct((batch, seq, D_MODEL), F32),
        compiler_params=_cparams("arbitrary", "arbitrary"),
        name="out_proj_mlp",
    )(x, og, od, mod, n2.reshape(1, 1, D_MODEL), final_g.reshape(1, 1, D_MODEL), w_out, w_up, w_down)


def _pack_w_in(w_in_l):
    a = 2 * GLA_QK + 2 * GLA_V
    main = jnp.concatenate([w_in_l[:, :a], w_in_l[:, a + GLA_RANK:]], axis=1)
    ar = jnp.pad(w_in_l[:, a:a + GLA_RANK], ((0, 0), (0, RANK_PAD - GLA_RANK)))
    return jnp.concatenate([main, ar], axis=1).astype(BF16)


def _trunk(x, mod_all, state0, cache_k, cache_v, params, bias):
    batch, seq, _ = x.shape
    prompt = cache_k is None
    if not prompt:
        cache_k = cache_k.reshape(DEPTH, batch, -1, DIFF_DV)
        cache_v = cache_v.reshape(DEPTH, batch, -1, DIFF_DV)
    new_k = new_v = None
    ss = []
    for l in range(DEPTH):
        p = params[l]
        lam_init = 0.8 - 0.6 * math.exp(-0.3 * l)
        mod = mod_all[l]
        qg, kg, vg, gg, la, qd, new_k, new_v, kdb, vdb = _in_proj(
            x, mod, p["n1"], p["w_all"], p["wa2p"], p["ba"], l, new_k, new_v, transposed_v=prompt)
        if state0 is None:
            s0 = jnp.zeros((batch, GLA_QK, GLA_DV), F32)
        else:
            s0 = state0[l].reshape(batch, GLA_QK, GLA_DV)
        og, s_new = _gla(qg, kg, la, vg, gg, s0, p["gla_g"])
        if prompt:
            od = _attn_prompt(qd, kdb, vdb, bias[1], p["lam_vecs"], p["diff_g"], lam_init)
        else:
            od = _attn_sample(qd, cache_k, cache_v, l, kdb, vdb, bias[0], p["lam_vecs"], p["diff_g"], lam_init)
        x = _out_mlp(x, og, od, mod, p["n2"], p["final_g"], p["w_out"], p["w_up"], p["w_down"], l == DEPTH - 1)
        ss.append(s_new.reshape(batch, HEADS, GLA_DK, GLA_DV))
    cache_shape = (DEPTH, batch, seq, HEADS, DIFF_DV)
    return x, new_k.reshape(cache_shape), new_v.reshape(cache_shape), jnp.stack(ss)


def kernel(x_prompt, x_sample, cache_k, cache_v, state_gla, c_prompt, c_sample, w_mod, b_mod, norm1_g, w_in, w_a2, b_a, gla_norm_g, lam_q1, lam_k1, lam_q2, lam_k2, diff_norm_g, rel_bias, w_out, norm2_g, w_up, w_down, final_g):
    nbp = x_prompt.shape[0]
    mod_all = _modulation(jnp.concatenate([c_prompt, c_sample], axis=0), w_mod, b_mod)
    mod_all = mod_all.reshape(DEPTH, -1, 6, D_MODEL)
    bias = _bias_tiles(rel_bias)
    params = []
    for l in range(DEPTH):
        params.append(dict(
            n1=norm1_g[l], n2=norm2_g[l], final_g=final_g,
            w_all=_pack_w_in(w_in[l]),
            wa2p=jnp.pad(w_a2[l], ((0, RANK_PAD - GLA_RANK), (0, 0))).astype(BF16),
            ba=b_a[l],
            gla_g=gla_norm_g[l],
            lam_vecs=jnp.stack([lam_q1[l], lam_k1[l], lam_q2[l], lam_k2[l]]),
            diff_g=diff_norm_g[l].reshape(1, DIFF_V),
            w_out=w_out[l].astype(BF16), w_up=w_up[l].astype(BF16), w_down=w_down[l].astype(BF16),
        ))
    y_p, k_p, v_p, s_p = _trunk(x_prompt, mod_all[:, :nbp], None, None, None, params, bias)
    y_s, k_s, v_s, s_s = _trunk(x_sample, mod_all[:, nbp:], state_gla, cache_k, cache_v, params, bias)
    return (y_p, y_s, k_p, v_p, s_p, k_s, v_s, s_s)
```

```python
import functools
import math

import jax
import jax.numpy as jnp
import numpy as np
from jax import lax
from jax.experimental import pallas as pl
from jax.experimental.pallas import tpu as pltpu

F32 = jnp.float32
BF16 = jnp.bfloat16

D_MODEL = 1024
DEPTH = 4
CHUNK = 64
HEADS = 4
GLA_DK = 64
GLA_DV = 128
GLA_RANK = 16
GLA_TAU = 16.0
DIFF_DK = 64
DIFF_DV = 128
N_BUCKETS = 32
D_FF = 4 * D_MODEL
EPS = 1e-6

GLA_QK = HEADS * GLA_DK
GLA_V = HEADS * GLA_DV
DIFF_QK = HEADS * 2 * DIFF_DK
DIFF_V = HEADS * DIFF_DV
RANK_PAD = 128
PROJ_W = 2 * GLA_QK + 2 * GLA_V + 2 * DIFF_QK + DIFF_V + RANK_PAD

V7X_VMEM_LIMIT_BYTES = 60000 * 1024
ATT_BLOCK = 256
ATT_PAIR = 2 * ATT_BLOCK
SAMPLE_KEY_BLOCK = 2048
FAR_BUCKET = 15
LOG2E = math.log2(math.e)
ATT_VROWS = DIFF_DV + 16
GLA_FAST_DECAY_LIMIT = 60.0
GLA_FAST_KEY_LIMIT = 1e9
GLA_FAST_SPAN = 256
CHUNK_SHIFT = 6
DK_SHIFT = 6
DV_SHIFT = 7


def _cparams(*sem):
    return pltpu.CompilerParams(dimension_semantics=sem, vmem_limit_bytes=V7X_VMEM_LIMIT_BYTES)


def _sigmoid(x):
    return 1.0 / (1.0 + jnp.exp(-x))


def _mod_kernel(c_ref, w_ref, b_ref, o_ref):
    c = c_ref[...]
    ca = (c * _sigmoid(c)).astype(BF16)
    o_ref[0] = jnp.dot(ca, w_ref[0].astype(BF16), preferred_element_type=F32) + b_ref[0]


def _modulation(c_all, w_mod, b_mod):
    nb = c_all.shape[0]
    ncol = 6 * D_MODEL // D_MODEL
    return pl.pallas_call(
        _mod_kernel,
        grid=(DEPTH, ncol),
        in_specs=[
            pl.BlockSpec((nb, D_MODEL), lambda l, j: (0, 0)),
            pl.BlockSpec((1, D_MODEL, D_MODEL), lambda l, j: (l, 0, j)),
            pl.BlockSpec((1, 1, D_MODEL), lambda l, j: (l, 0, j)),
        ],
        out_specs=pl.BlockSpec((1, nb, D_MODEL), lambda l, j: (l, 0, j)),
        out_shape=jax.ShapeDtypeStruct((DEPTH, nb, 6 * D_MODEL), F32),
        compiler_params=_cparams("arbitrary", "arbitrary"),
        name="adaln_modulation",
    )(c_all, w_mod, b_mod.reshape(DEPTH, 1, 6 * D_MODEL))


_BUCKET_THRESHOLDS = (12, 16, 23, 32, 46, 64, 91)


def _bias_kernel(rb_ref, o_ref):
    orient = pl.program_id(0)
    t = pl.program_id(1)
    h = pl.program_id(2)
    r = lax.broadcasted_iota(jnp.int32, (ATT_BLOCK, ATT_BLOCK), 0)
    c = lax.broadcasted_iota(jnp.int32, (ATT_BLOCK, ATT_BLOCK), 1)
    i = r + orient * (c - r)
    j = c + orient * (r - c)
    rel = j - i - t * ATT_BLOCK
    n = jnp.abs(rel)
    large = jnp.full_like(n, 8)
    for thr in _BUCKET_THRESHOLDS:
        large = large + jnp.where(n >= thr, 1, 0)
    bucket = jnp.where(rel > 0, N_BUCKETS // 2, 0) + jnp.where(n < 8, n, large)
    far = rb_ref[FAR_BUCKET, h]
    val = jnp.zeros((ATT_BLOCK, ATT_BLOCK), F32)
    for b in range(N_BUCKETS):
        val = jnp.where(bucket == b, (rb_ref[b, h] - far) * LOG2E, val)
    visible = (j >> CHUNK_SHIFT) <= (i >> CHUNK_SHIFT) + t * ATT_BLOCK
    o_ref[0, 0, 0] = jnp.where(visible, val, -jnp.inf)


def _bias_tiles(rel_bias):
    return pl.pallas_call(
        _bias_kernel,
        grid=(2, 2, HEADS),
        in_specs=[pl.BlockSpec(memory_space=pltpu.SMEM)],
        out_specs=pl.BlockSpec((1, 1, 1, ATT_BLOCK, ATT_BLOCK), lambda o, t, h: (o, t, h, 0, 0)),
        out_shape=jax.ShapeDtypeStruct((2, 2, HEADS, ATT_BLOCK, ATT_BLOCK), F32),
        compiler_params=_cparams("arbitrary", "arbitrary", "arbitrary"),
        name="t5_bias_tiles",
    )(rel_bias)


def _in_proj_kernel(transposed_v, x_ref, mod_ref, n1_ref, w_ref, wa2_ref, ba_ref, k_in_ref, v_in_ref,
                    qg_ref, kg_ref, vg_ref, gg_ref, la_ref, qd_ref, kd_ref, vd_ref, kdb_ref, vdb_ref):
    nb, tt, _ = x_ref.shape
    x = x_ref[...]
    ms = jnp.mean(x * x, axis=-1, keepdims=True)
    y = x * lax.rsqrt(ms + EPS) * n1_ref[...]
    mod = mod_ref[...]
    h = y * (1.0 + mod[:, 1:2, :]) + mod[:, 0:1, :]
    hb = h.reshape(nb * tt, D_MODEL).astype(BF16)
    proj = jnp.dot(hb, w_ref[...], preferred_element_type=F32)

    def put(ref, lo, width, scale=None):
        v = proj[:, lo:lo + width]
        if scale is not None:
            v = v * scale
        ref[...] = v.reshape(nb, tt, width).astype(ref.dtype)
        return lo + width

    lo = put(qg_ref, 0, GLA_QK, GLA_DK ** -0.5)
    lo = put(kg_ref, lo, GLA_QK)
    lo = put(vg_ref, lo, GLA_V)
    lo = put(gg_ref, lo, GLA_V)
    lo = put(qd_ref, lo, DIFF_QK, DIFF_DK ** -0.5 * LOG2E)

    def put_cache(ref, lo):
        for h in range(HEADS):
            slab = proj[:, lo + h * DIFF_DV:lo + (h + 1) * DIFF_DV]
            for b in range(nb):
                ref[0, b, pl.ds(h, tt, stride=HEADS), :] = slab[b * tt:(b + 1) * tt, :]

    put_cache(kd_ref, lo)
    lo = put(kdb_ref, lo, DIFF_QK)
    put_cache(vd_ref, lo)
    if transposed_v:
        pad_row = lax.broadcasted_iota(jnp.int32, (ATT_VROWS - DIFF_DV, nb * tt), 0)
        ones_row = jnp.where(pad_row == 0, 1.0, 0.0).astype(vdb_ref.dtype)
        for h in range(HEADS):
            vt = jnp.transpose(proj[:, lo + h * DIFF_DV:lo + (h + 1) * DIFF_DV]).astype(vdb_ref.dtype)
            vdb_ref[0, h * ATT_VROWS:h * ATT_VROWS + DIFF_DV, :] = vt
            vdb_ref[0, h * ATT_VROWS + DIFF_DV:(h + 1) * ATT_VROWS, :] = ones_row
        lo = lo + DIFF_V
    else:
        lo = put(vdb_ref, lo, DIFF_V)
    ar = proj[:, lo:lo + RANK_PAD].astype(BF16)
    z = jnp.dot(ar, wa2_ref[...], preferred_element_type=F32) + ba_ref[...]
    la = (jnp.minimum(z, 0.0) - jnp.log(1.0 + jnp.exp(-jnp.abs(z)))) * (1.0 / GLA_TAU)
    la_ref[...] = la.reshape(nb, tt, GLA_QK)


def _row_tiling(batch, seq, target):
    if seq >= target:
        assert seq % target == 0
        return 1, target
    nb = min(batch, max(1, target // seq))
    while batch % nb:
        nb -= 1
    return nb, seq


def _in_proj(x, mod, n1, w_all, wa2p, ba, layer, new_k, new_v, transposed_v):
    batch, seq, _ = x.shape
    nb, tt = (1, min(seq, 512)) if transposed_v else _row_tiling(batch, seq, 512)
    assert seq % tt == 0
    grid = (batch // nb, seq // tt)
    row = lambda w: pl.BlockSpec((nb, tt, w), lambda b, t: (b, t, 0))
    const = lambda shape: pl.BlockSpec(shape, lambda b, t: (0,) * len(shape))
    cache = pl.BlockSpec((1, nb, tt * HEADS, DIFF_DV), lambda b, t: (layer, b, t, 0))
    cache_shape = jax.ShapeDtypeStruct((DEPTH, batch, seq * HEADS, DIFF_DV), F32)
    sds = lambda w, dt: jax.ShapeDtypeStruct((batch, seq, w), dt)
    if transposed_v:
        vb_spec = pl.BlockSpec((1, HEADS * ATT_VROWS, tt), lambda b, t: (b, 0, t))
        vb_shape = jax.ShapeDtypeStruct((batch, HEADS * ATT_VROWS, seq), BF16)
    else:
        vb_spec, vb_shape = row(DIFF_V), sds(DIFF_V, BF16)
    out_specs = [row(GLA_QK), row(GLA_QK), row(GLA_V), row(GLA_V), row(GLA_QK), row(DIFF_QK),
                 cache, cache, row(DIFF_QK), vb_spec]
    out_shape = [sds(GLA_QK, F32), sds(GLA_QK, F32), sds(GLA_V, BF16), sds(GLA_V, F32), sds(GLA_QK, F32),
                 sds(DIFF_QK, BF16), cache_shape, cache_shape, sds(DIFF_QK, BF16), vb_shape]
    in_specs = [
        row(D_MODEL),
        pl.BlockSpec((nb, 6, D_MODEL), lambda b, t: (b, 0, 0)),
        const((1, 1, D_MODEL)),
        const((D_MODEL, PROJ_W)),
        const((RANK_PAD, GLA_QK)),
        const((1, GLA_QK)),
        pl.BlockSpec(memory_space=pl.ANY),
        pl.BlockSpec(memory_space=pl.ANY),
    ]
    return pl.pallas_call(
        functools.partial(_in_proj_kernel, transposed_v),
        grid=grid,
        in_specs=in_specs,
        out_specs=out_specs,
        out_shape=out_shape,
        input_output_aliases={6: 6, 7: 7},
        compiler_params=_cparams("arbitrary", "arbitrary"),
        name="in_projection",
    )(x, mod, n1.reshape(1, 1, D_MODEL), w_all, wa2p, ba.reshape(1, GLA_QK), new_k, new_v)


def _stack_heads(a, width):
    return jnp.concatenate([a[:, h * width:(h + 1) * width] for h in range(HEADS)], axis=0)


def _unstack_heads(a, rows):
    return jnp.concatenate([a[h * rows:(h + 1) * rows, :] for h in range(HEADS)], axis=1)


def _gla_kernel(q_ref, k_ref, la_ref, v_ref, gg_ref, s0_ref, g_ref, og_ref, sout_ref, s_scr, b_scr, v_scr):
    t = pl.program_id(1)
    rows = q_ref.shape[1]
    span = min(rows, GLA_FAST_SPAN)
    assert rows % span == 0 and span & (span - 1) == 0

    @pl.when(t == 0)
    def _():
        s_scr[...] = s0_ref[0]

    def masks(n):
        r = lax.broadcasted_iota(jnp.int32, (n, n), 0)
        c = lax.broadcasted_iota(jnp.int32, (n, n), 1)
        tril = jnp.where(c <= r, 1.0, 0.0).astype(BF16)
        rq = lax.broadcasted_iota(jnp.int32, (HEADS * n, GLA_QK), 0)
        cq = lax.broadcasted_iota(jnp.int32, (HEADS * n, GLA_QK), 1)
        head_lanes = (rq >> (n.bit_length() - 1)) == (cq >> DK_SHIFT)
        ra = lax.broadcasted_iota(jnp.int32, (HEADS * n, n), 0)
        ca = lax.broadcasted_iota(jnp.int32, (HEADS * n, n), 1)
        return tril, head_lanes, ca <= (ra & (n - 1))

    span_decay = jnp.sum(la_ref[0].reshape(rows // span, span, GLA_QK), axis=1)
    fast = jnp.logical_and(jnp.min(span_decay) >= -GLA_FAST_DECAY_LIMIT,
                           jnp.max(jnp.abs(k_ref[0])) <= GLA_FAST_KEY_LIMIT)

    def head_rows(a, h, n):
        return a[h * n:(h + 1) * n, :]

    def intra_fast(n, causal):
        def intra(base, q, k, v, b, qstack):
            kt = (k * jnp.exp(-b)).astype(BF16)
            a = lax.dot_general(qstack, kt, _NT, preferred_element_type=F32)
            a = jnp.where(causal, a, 0.0).astype(BF16)
            return jnp.concatenate(
                [jnp.dot(head_rows(a, h, n), v[:, h * GLA_DV:(h + 1) * GLA_DV], preferred_element_type=F32)
                 for h in range(HEADS)], axis=0)
        return intra

    def intra_exact(n):
        row_i = lax.broadcasted_iota(jnp.int32, (n, GLA_QK), 0)
        re = lax.broadcasted_iota(jnp.int32, (GLA_QK, GLA_V), 0)
        ce = lax.broadcasted_iota(jnp.int32, (GLA_QK, GLA_V), 1)
        expand = jnp.where((re >> DK_SHIFT) == (ce >> DV_SHIFT), 1.0, 0.0).astype(BF16)

        def intra(base, q, k, v, b, qstack):
            b_scr[...] = b
            v_scr[...] = v.astype(F32)

            def key_row(j, acc):
                kj = k_ref[0, pl.ds(base + j, 1), :]
                bj = b_scr[pl.ds(j, 1), :]
                vj = v_scr[pl.ds(j, 1), :]
                w = jnp.where(row_i >= j, jnp.exp(jnp.minimum(b - bj, 0.0)), 0.0) * q * kj
                return acc + jnp.dot(w.astype(BF16), expand, preferred_element_type=F32) * vj

            acc = lax.fori_loop(0, n, key_row, jnp.zeros((n, GLA_V), F32))
            return _stack_heads(acc, GLA_DV)
        return intra

    def process(n, base, s_old, tril, head_lanes, intra):
        q = q_ref[0, pl.ds(base, n), :]
        k = k_ref[0, pl.ds(base, n), :]
        la = la_ref[0, pl.ds(base, n), :]
        v = v_ref[0, pl.ds(base, n), :]
        la1 = la.astype(BF16)
        la2 = (la - la1.astype(F32)).astype(BF16)
        b = jnp.dot(tril, la1, preferred_element_type=F32) + jnp.dot(tril, la2, preferred_element_type=F32)
        b_last = b[n - 1:n, :]
        qt = (q * jnp.exp(b)).astype(BF16)
        kst = jnp.transpose(k * jnp.exp(b_last - b)).astype(BF16)
        qstack = jnp.where(head_lanes, jnp.concatenate([qt] * HEADS, axis=0), jnp.zeros((), BF16))
        o = jnp.dot(qstack, s_old.astype(BF16), preferred_element_type=F32)
        o = o + intra(base, q, k, v, b, qstack)
        gain = jnp.concatenate([jnp.broadcast_to(g_ref[h:h + 1, :], (n, GLA_DV)) for h in range(HEADS)], axis=0)
        og = o * lax.rsqrt(jnp.mean(o * o, axis=-1, keepdims=True) + EPS) * gain
        gate = _stack_heads(gg_ref[0, pl.ds(base, n), :], GLA_DV)
        og = og * (gate * _sigmoid(gate))
        og_ref[0, pl.ds(base, n), :] = _unstack_heads(og, n).astype(og_ref.dtype)
        ds = jnp.concatenate(
            [jnp.dot(head_rows(kst, h, GLA_DK), v[:, h * GLA_DV:(h + 1) * GLA_DV], preferred_element_type=F32)
             for h in range(HEADS)], axis=0)
        decay = jnp.transpose(jnp.broadcast_to(jnp.exp(b_last), (GLA_DV, GLA_QK)))
        return decay * s_old + ds

    def fast_step(s):
        tril, head_lanes, causal = masks(span)
        for i in range(rows // span):
            s = process(span, i * span, s, tril, head_lanes, intra_fast(span, causal))
        return s

    def exact_step(s):
        tril, head_lanes, _ = masks(CHUNK)
        intra = intra_exact(CHUNK)
        return lax.fori_loop(
            0, rows // CHUNK,
            lambda c, s: process(CHUNK, pl.multiple_of(c * CHUNK, CHUNK), s, tril, head_lanes, intra), s)

    s_scr[...] = lax.cond(fast, fast_step, exact_step, s_scr[...])

    @pl.when(t == pl.num_programs(1) - 1)
    def _():
        sout_ref[0] = s_scr[...]


def _gla(qg, kg, la, vg, gg, s0, gain):
    batch, seq, _ = qg.shape
    rows = min(seq, 512)
    assert seq % rows == 0 and rows % CHUNK == 0
    grid = (batch, seq // rows)
    row = lambda w: pl.BlockSpec((1, rows, w), lambda b, t: (b, t, 0))
    state = pl.BlockSpec((1, GLA_QK, GLA_DV), lambda b, t: (b, 0, 0))
    return pl.pallas_call(
        _gla_kernel,
        grid=grid,
        in_specs=[row(GLA_QK), row(GLA_QK), row(GLA_QK), row(GLA_V), row(GLA_V), state,
                  pl.BlockSpec((HEADS, GLA_DV), lambda b, t: (0, 0))],
        out_specs=[row(GLA_V), state],
        out_shape=[jax.ShapeDtypeStruct((batch, seq, GLA_V), BF16),
                   jax.ShapeDtypeStruct((batch, GLA_QK, GLA_DV), F32)],
        scratch_shapes=[pltpu.VMEM((GLA_QK, GLA_DV), F32),
                        pltpu.VMEM((CHUNK, GLA_QK), F32),
                        pltpu.VMEM((CHUNK, GLA_V), F32)],
        compiler_params=_cparams("arbitrary", "arbitrary"),
        name="gla_chunks",
    )(qg, kg, la, vg, gg, s0, gain)


def _split_maps(qh):
    lane = lax.broadcasted_iota(jnp.int32, qh.shape, 1)
    zero = jnp.zeros_like(qh)
    return jnp.concatenate([jnp.where(lane < DIFF_DK, qh, zero), jnp.where(lane >= DIFF_DK, qh, zero)], axis=0)


def _diff_lambda(lam_ref, lam_init):
    lv = lam_ref[...]
    s1 = jnp.sum(lv[0:1, :] * lv[1:2, :], axis=1, keepdims=True)
    s2 = jnp.sum(lv[2:3, :] * lv[3:4, :], axis=1, keepdims=True)
    return jnp.exp(s1) - jnp.exp(s2) + lam_init


_NT = (((1,), (1,)), ((), ()))


def _prompt_scores(k_ref, q2, base, size, dst):
    for h in range(HEADS):
        cols = slice(h * 2 * DIFF_DK, (h + 1) * 2 * DIFF_DK)
        dst[h][0:size, :] = lax.dot_general(k_ref[0, pl.ds(base, size), cols], q2[h], _NT,
                                            preferred_element_type=F32)


def _prompt_softmax_step(src, size, vt_ref, base, bias_t_ref, tiles, m_ref, acc_ref):
    for h in range(HEADS):
        s = src[h][0:size, :]
        if tiles is not None:
            parts = []
            for i, tile in enumerate(tiles):
                part = s[i * ATT_BLOCK:(i + 1) * ATT_BLOCK, :]
                if tile is not None:
                    part = part + jnp.concatenate([bias_t_ref[tile, h]] * 2, axis=1)
                parts.append(part)
            s = jnp.concatenate(parts, axis=0)
        m_prev = m_ref[h]
        m_next = jnp.maximum(m_prev, jnp.max(s, axis=0, keepdims=True))
        p = jnp.exp2(s - m_next)
        alpha = jnp.exp2(m_prev - m_next)
        vtb = vt_ref[0, h * ATT_VROWS:(h + 1) * ATT_VROWS, pl.ds(base, size)]
        acc_ref[h] = alpha * acc_ref[h] + jnp.dot(vtb, p.astype(BF16), preferred_element_type=F32)
        m_ref[h] = m_next


def _attn_prompt_kernel(lam_init, q_ref, k_ref, vt_ref, bias_ref, lam_ref, g_ref, o_ref, m_ref, acc_ref, *scr):
    s_scr = scr
    qi = pl.program_id(1)
    tq = q_ref.shape[1]
    lam = _diff_lambda(lam_ref, lam_init)
    m_ref[...] = jnp.full(m_ref.shape, -jnp.inf, F32)
    acc_ref[...] = jnp.zeros(acc_ref.shape, F32)
    q2 = [_split_maps(q_ref[0, :, h * 2 * DIFF_DK:(h + 1) * 2 * DIFF_DK]) for h in range(HEADS)]
    nfar = jnp.maximum(qi - 1, 0)
    npair = nfar // 2
    odd = nfar - 2 * npair

    def far_pair(p, carry):
        base = pl.multiple_of(p * ATT_PAIR, ATT_PAIR)
        _prompt_scores(k_ref, q2, base, ATT_PAIR, s_scr)
        _prompt_softmax_step(s_scr, ATT_PAIR, vt_ref, base, bias_ref, None, m_ref, acc_ref)
        return carry

    lax.fori_loop(0, npair, far_pair, 0)
    tail = pl.multiple_of(npair * ATT_PAIR, ATT_BLOCK)

    def tail_step(tiles):
        def run(_):
            size = len(tiles) * ATT_BLOCK
            _prompt_scores(k_ref, q2, tail, size, s_scr)
            _prompt_softmax_step(s_scr, size, vt_ref, tail, bias_ref, tiles, m_ref, acc_ref)
            return 0
        return run

    lax.switch(jnp.where(qi == 0, 0, 1 + odd), [tail_step((0,)), tail_step((1, 0)), tail_step((None, 1, 0))], 0)
    for h in range(HEADS):
        a = acc_ref[h]
        out = a[:DIFF_DV, :] * (1.0 / a[DIFF_DV:DIFF_DV + 1, :])
        od = out[:, :tq] - lam * out[:, tq:]
        od = od * lax.rsqrt(jnp.mean(od * od, axis=0, keepdims=True) + EPS)
        vcols = slice(h * DIFF_DV, (h + 1) * DIFF_DV)
        od = jnp.transpose(od) * g_ref[:, vcols] * (1.0 - lam_init)
        o_ref[0, :, vcols] = od.astype(o_ref.dtype)


def _attn_prompt(qd, kdb, vtb, bias_t, lam_vecs, g_row, lam_init):
    batch, seq, _ = qd.shape
    tq = ATT_BLOCK
    assert seq % tq == 0 and seq >= ATT_PAIR
    return pl.pallas_call(
        functools.partial(_attn_prompt_kernel, lam_init),
        grid=(batch, seq // tq),
        in_specs=[
            pl.BlockSpec((1, tq, DIFF_QK), lambda b, i: (b, i, 0)),
            pl.BlockSpec((1, seq, DIFF_QK), lambda b, i: (b, 0, 0)),
            pl.BlockSpec((1, HEADS * ATT_VROWS, seq), lambda b, i: (b, 0, 0)),
            pl.BlockSpec((2, HEADS, ATT_BLOCK, ATT_BLOCK), lambda b, i: (0, 0, 0, 0)),
            pl.BlockSpec((4, DIFF_DK), lambda b, i: (0, 0)),
            pl.BlockSpec((1, DIFF_V), lambda b, i: (0, 0)),
        ],
        out_specs=pl.BlockSpec((1, tq, DIFF_V), lambda b, i: (b, i, 0)),
        out_shape=jax.ShapeDtypeStruct((batch, seq, DIFF_V), BF16),
        scratch_shapes=[pltpu.VMEM((HEADS, 1, 2 * tq), F32), pltpu.VMEM((HEADS, ATT_VROWS, 2 * tq), F32)]
                       + [pltpu.VMEM((3 * ATT_BLOCK, 2 * tq), F32)] * HEADS,
        compiler_params=_cparams("arbitrary", "arbitrary"),
        name="diff_attention_prompt",
    )(qd, kdb, vtb, bias_t, lam_vecs, g_row)


def _with_ones_column(v):
    lane = lax.broadcasted_iota(jnp.int32, v.shape, 1)
    return jnp.concatenate([v, jnp.where(lane == 0, 1.0, 0.0).astype(v.dtype)], axis=1)


def _sample_softmax_step(s, v_aug, m_ref, acc_ref, h):
    m_prev = m_ref[h]
    m_next = jnp.maximum(m_prev, jnp.max(s, axis=1, keepdims=True))
    p = jnp.exp2(s - m_next[:, :1])
    alpha = jnp.exp2(m_prev - m_next)
    acc_ref[h] = (jnp.concatenate([alpha, alpha], axis=1) * acc_ref[h]
                  + jnp.dot(p.astype(BF16), v_aug, preferred_element_type=F32))
    m_ref[h] = m_next


def _attn_sample_kernel(lam_init, q_ref, ck_ref, cv_ref, nk_ref, nv_ref, bias_ref, lam_ref, g_ref, o_ref,
                        m_ref, acc_ref, *s_scr):
    step = pl.program_id(1)
    nstep = pl.num_programs(1)
    tq = q_ref.shape[1]
    tk = ck_ref.shape[2] // HEADS

    def head_rows(ref, h):
        return ref[0, 0, pl.ds(h, tk, stride=HEADS), :].astype(BF16)

    @pl.when(step == 0)
    def _():
        m_ref[...] = jnp.full(m_ref.shape, -jnp.inf, F32)
        acc_ref[...] = jnp.zeros(acc_ref.shape, F32)

    q2 = [_split_maps(q_ref[0, :, h * 2 * DIFF_DK:(h + 1) * 2 * DIFF_DK]) for h in range(HEADS)]
    for h in range(HEADS):
        s_scr[h][...] = lax.dot_general(q2[h], head_rows(ck_ref, h), _NT, preferred_element_type=F32)

    def cache_block(near):
        for h in range(HEADS):
            s = s_scr[h][...]
            if near:
                bias = jnp.concatenate([bias_ref[1, h, :tq, :]] * 2, axis=0)
                s = jnp.concatenate([s[:, :tk - ATT_BLOCK], s[:, tk - ATT_BLOCK:] + bias], axis=1)
            _sample_softmax_step(s, _with_ones_column(head_rows(cv_ref, h)), m_ref, acc_ref, h)

    @pl.when(step < nstep - 1)
    def _():
        cache_block(False)

    @pl.when(step == nstep - 1)
    def _():
        cache_block(True)
        lam = _diff_lambda(lam_ref, lam_init)
        for h in range(HEADS):
            cols = slice(h * 2 * DIFF_DK, (h + 1) * 2 * DIFF_DK)
            vcols = slice(h * DIFF_DV, (h + 1) * DIFF_DV)
            s = lax.dot_general(q2[h], nk_ref[0, :, cols], _NT, preferred_element_type=F32)
            s = s + jnp.concatenate([bias_ref[0, h, :tq, :tq]] * 2, axis=0)
            _sample_softmax_step(s, _with_ones_column(nv_ref[0, :, vcols]), m_ref, acc_ref, h)
            a = acc_ref[h]
            out = a[:, :DIFF_DV] * (1.0 / a[:, DIFF_DV:DIFF_DV + 1])
            od = out[:tq, :] - lam * out[tq:, :]
            od = od * lax.rsqrt(jnp.mean(od * od, axis=-1, keepdims=True) + EPS) * g_ref[:, vcols]
            o_ref[0, :, vcols] = (od * (1.0 - lam_init)).astype(o_ref.dtype)


def _attn_sample(qd, cache_k, cache_v, layer, kdb, vdb, bias, lam_vecs, g_row, lam_init):
    batch, tq, _ = qd.shape
    past = cache_k.shape[2] // HEADS
    assert tq == CHUNK and past % CHUNK == 0, "sample queries must be one chunk, appended at a chunk boundary"
    tk = min(past, SAMPLE_KEY_BLOCK)
    assert past % tk == 0 and tk % ATT_BLOCK == 0
    new = lambda w: pl.BlockSpec((1, tq, w), lambda b, s: (b, 0, 0))
    blk = pl.BlockSpec((1, 1, tk * HEADS, DIFF_DV), lambda b, s: (layer, b, s, 0))
    return pl.pallas_call(
        functools.partial(_attn_sample_kernel, lam_init),
        grid=(batch, past // tk),
        in_specs=[
            new(DIFF_QK), blk, blk, new(DIFF_QK), new(DIFF_V),
            pl.BlockSpec((2, HEADS, ATT_BLOCK, ATT_BLOCK), lambda b, s: (0, 0, 0, 0)),
            pl.BlockSpec((4, DIFF_DK), lambda b, s: (0, 0)),
            pl.BlockSpec((1, DIFF_V), lambda b, s: (0, 0)),
        ],
        out_specs=new(DIFF_V),
        out_shape=jax.ShapeDtypeStruct((batch, tq, DIFF_V), BF16),
        scratch_shapes=[pltpu.VMEM((HEADS, 2 * tq, DIFF_DV), F32), pltpu.VMEM((HEADS, 2 * tq, 2 * DIFF_DV), F32)]
                       + [pltpu.VMEM((2 * tq, tk), F32)] * HEADS,
        compiler_params=_cparams("arbitrary", "arbitrary"),
        name="diff_attention_sample",
    )(qd, cache_k, cache_v, kdb, vdb, bias, lam_vecs, g_row)


def _out_mlp_kernel(final, x_ref, og_ref, od_ref, mod_ref, n2_ref, fg_ref, wo_ref, wu_ref, wd_ref, o_ref):
    nb, tt, _ = x_ref.shape
    rows = nb * tt
    mod = mod_ref[...]
    o = jnp.concatenate([og_ref[...].reshape(rows, GLA_V), od_ref[...].reshape(rows, DIFF_V)], axis=1)
    attn = jnp.dot(o, wo_ref[...], preferred_element_type=F32).reshape(nb, tt, D_MODEL)
    x1 = x_ref[...] + mod[:, 2:3, :] * attn
    ms = jnp.mean(x1 * x1, axis=-1, keepdims=True)
    h2 = x1 * lax.rsqrt(ms + EPS) * n2_ref[...]
    h2 = (h2 * (1.0 + mod[:, 4:5, :]) + mod[:, 3:4, :]).reshape(rows, D_MODEL).astype(BF16)
    ff = jnp.zeros((rows, D_MODEL), F32)
    for c in range(D_FF // D_MODEL):
        cs = slice(c * D_MODEL, (c + 1) * D_MODEL)
        u = jnp.maximum(jnp.dot(h2, wu_ref[:, cs], preferred_element_type=F32), 0.0)
        ff = ff + jnp.dot((u * u).astype(BF16), wd_ref[cs, :], preferred_element_type=F32)
    x2 = x1 + mod[:, 5:6, :] * ff.reshape(nb, tt, D_MODEL)
    if final:
        ms2 = jnp.mean(x2 * x2, axis=-1, keepdims=True)
        x2 = x2 * lax.rsqrt(ms2 + EPS) * fg_ref[...]
    o_ref[...] = x2


def _out_mlp(x, og, od, mod, n2, final_g, w_out, w_up, w_down, final):
    batch, seq, _ = x.shape
    nb, tt = _row_tiling(batch, seq, 512)
    row = lambda w: pl.BlockSpec((nb, tt, w), lambda b, t: (b, t, 0))
    const = lambda shape: pl.BlockSpec(shape, lambda b, t: (0,) * len(shape), pipeline_mode=pl.Buffered(1))
    return pl.pallas_call(
        functools.partial(_out_mlp_kernel, final),
        grid=(batch // nb, seq // tt),
        in_specs=[
            row(D_MODEL), row(GLA_V), row(DIFF_V),
            pl.BlockSpec((nb, 6, D_MODEL), lambda b, t: (b, 0, 0)),
            const((1, 1, D_MODEL)), const((1, 1, D_MODEL)),
            const((D_MODEL, D_MODEL)), const((D_MODEL, D_FF)), const((D_FF, D_MODEL)),
        ],
        out_specs=row(D_MODEL),
        out_shape=jax.ShapeDtypeStruct((batch, seq, D_MODEL), F32),
        compiler_params=_cparams("arbitrary", "arbitrary"),
        name="out_proj_mlp",
    )(x, og, od, mod, n2.reshape(1, 1, D_MODEL), final_g.reshape(1, 1, D_MODEL), w_out, w_up, w_down)


def _pack_w_in(w_in_l):
    a = 2 * GLA_QK + 2 * GLA_V
    main = jnp.concatenate([w_in_l[:, :a], w_in_l[:, a + GLA_RANK:]], axis=1)
    ar = jnp.pad(w_in_l[:, a:a + GLA_RANK], ((0, 0), (0, RANK_PAD - GLA_RANK)))
    return jnp.concatenate([main, ar], axis=1).astype(BF16)


def _trunk(x, mod_all, state0, cache_k, cache_v, params, bias):
    batch, seq, _ = x.shape
    prompt = cache_k is None
    if not prompt:
        cache_k = cache_k.reshape(DEPTH, batch, -1, DIFF_DV)
        cache_v = cache_v.reshape(DEPTH, batch, -1, DIFF_DV)
    new_k = jnp.zeros((DEPTH, batch, seq * HEADS, DIFF_DV), F32)
    new_v = jnp.zeros((DEPTH, batch, seq * HEADS, DIFF_DV), F32)
    ss = []
    for l in range(DEPTH):
        p = params[l]
        lam_init = 0.8 - 0.6 * math.exp(-0.3 * l)
        mod = mod_all[l]
        qg, kg, vg, gg, la, qd, new_k, new_v, kdb, vdb = _in_proj(
            x, mod, p["n1"], p["w_all"], p["wa2p"], p["ba"], l, new_k, new_v, transposed_v=prompt)
        if state0 is None:
            s0 = jnp.zeros((batch, GLA_QK, GLA_DV), F32)
        else:
            s0 = state0[l].reshape(batch, GLA_QK, GLA_DV)
        og, s_new = _gla(qg, kg, la, vg, gg, s0, p["gla_g"])
        if prompt:
            od = _attn_prompt(qd, kdb, vdb, bias[1], p["lam_vecs"], p["diff_g"], lam_init)
        else:
            od = _attn_sample(qd, cache_k, cache_v, l, kdb, vdb, bias[0], p["lam_vecs"], p["diff_g"], lam_init)
        x = _out_mlp(x, og, od, mod, p["n2"], p["final_g"], p["w_out"], p["w_up"], p["w_down"], l == DEPTH - 1)
        ss.append(s_new.reshape(batch, HEADS, GLA_DK, GLA_DV))
    cache_shape = (DEPTH, batch, seq, HEADS, DIFF_DV)
    return x, new_k.reshape(cache_shape), new_v.reshape(cache_shape), jnp.stack(ss)


def kernel(x_prompt, x_sample, cache_k, cache_v, state_gla, c_prompt, c_sample, w_mod, b_mod, norm1_g, w_in, w_a2, b_a, gla_norm_g, lam_q1, lam_k1, lam_q2, lam_k2, diff_norm_g, rel_bias, w_out, norm2_g, w_up, w_down, final_g):
    nbp = x_prompt.shape[0]
    mod_all = _modulation(jnp.concatenate([c_prompt, c_sample], axis=0), w_mod, b_mod)
    mod_all = mod_all.reshape(DEPTH, -1, 6, D_MODEL)
    bias = _bias_tiles(rel_bias)
    params = []
    for l in range(DEPTH):
        params.append(dict(
            n1=norm1_g[l], n2=norm2_g[l], final_g=final_g,
            w_all=_pack_w_in(w_in[l]),
            wa2p=jnp.pad(w_a2[l], ((0, RANK_PAD - GLA_RANK), (0, 0))).astype(BF16),
            ba=b_a[l],
            gla_g=gla_norm_g[l],
            lam_vecs=jnp.stack([lam_q1[l], lam_k1[l], lam_q2[l], lam_k2[l]]),
            diff_g=diff_norm_g[l].reshape(1, DIFF_V),
            w_out=w_out[l].astype(BF16), w_up=w_up[l].astype(BF16), w_down=w_down[l].astype(BF16),
        ))
    y_p, k_p, v_p, s_p = _trunk(x_prompt, mod_all[:, :nbp], None, None, None, params, bias)
    y_s, k_s, v_s, s_s = _trunk(x_sample, mod_all[:, nbp:], state_gla, cache_k, cache_v, params, bias)
    return (y_p, y_s, k_p, v_p, s_p, k_s, v_s, s_s)
```

```python
import functools
import math

import jax
import jax.numpy as jnp
from jax import lax
from jax.experimental import pallas as pl
from jax.experimental.pallas import tpu as pltpu

F32 = jnp.float32
BF16 = jnp.bfloat16

D_MODEL = 1024
DEPTH = 4
CHUNK = 64
HEADS = 4
GLA_DK = 64
GLA_DV = 128
GLA_RANK = 16
GLA_TAU = 16.0
DIFF_DK = 64
DIFF_DV = 128
N_BUCKETS = 32
D_FF = 4 * D_MODEL
EPS = 1e-6

GLA_QK = HEADS * GLA_DK
GLA_V = HEADS * GLA_DV
DIFF_QK = HEADS * 2 * DIFF_DK
DIFF_V = HEADS * DIFF_DV
RANK_PAD = 128
PROJ_W = 2 * GLA_QK + 2 * GLA_V + 2 * DIFF_QK + DIFF_V + RANK_PAD

V7X_VMEM_LIMIT_BYTES = 60000 * 1024
ATT_BLOCK = 256
ATT_PAIR = 2 * ATT_BLOCK
ATT_QUAD = 4 * ATT_BLOCK
SAMPLE_KEY_BLOCK = 2048
FAR_BUCKET = 15
LOG2E = math.log2(math.e)
ATT_VROWS = DIFF_DV + 16
GLA_FAST_DECAY_LIMIT = 60.0
GLA_FAST_KEY_LIMIT = 1e9
GLA_FAST_SPAN = 256
CHUNK_SHIFT = 6
DK_SHIFT = 6
DV_SHIFT = 7


def _cparams(*sem):
    return pltpu.CompilerParams(dimension_semantics=sem, vmem_limit_bytes=V7X_VMEM_LIMIT_BYTES)


def _sigmoid(x):
    return 1.0 / (1.0 + jnp.exp(-x))


def _mod_kernel(c_ref, w_ref, b_ref, o_ref):
    c = c_ref[...]
    ca = (c * _sigmoid(c)).astype(BF16)
    o_ref[0] = jnp.dot(ca, w_ref[0].astype(BF16), preferred_element_type=F32) + b_ref[0]


def _modulation(c_all, w_mod, b_mod):
    nb = c_all.shape[0]
    return pl.pallas_call(
        _mod_kernel,
        grid=(DEPTH, 6),
        in_specs=[
            pl.BlockSpec((nb, D_MODEL), lambda l, j: (0, 0)),
            pl.BlockSpec((1, D_MODEL, D_MODEL), lambda l, j: (l, 0, j)),
            pl.BlockSpec((1, 1, D_MODEL), lambda l, j: (l, 0, j)),
        ],
        out_specs=pl.BlockSpec((1, nb, D_MODEL), lambda l, j: (l, 0, j)),
        out_shape=jax.ShapeDtypeStruct((DEPTH, nb, 6 * D_MODEL), F32),
        compiler_params=_cparams("arbitrary", "arbitrary"),
        name="adaln_modulation",
    )(c_all, w_mod, b_mod.reshape(DEPTH, 1, 6 * D_MODEL))


_BUCKET_THRESHOLDS = (12, 16, 23, 32, 46, 64, 91)


def _bias_kernel(rb_ref, o_ref):
    orient = pl.program_id(0)
    t = pl.program_id(1)
    h = pl.program_id(2)
    r = lax.broadcasted_iota(jnp.int32, (ATT_BLOCK, ATT_BLOCK), 0)
    c = lax.broadcasted_iota(jnp.int32, (ATT_BLOCK, ATT_BLOCK), 1)
    i = r + orient * (c - r)
    j = c + orient * (r - c)
    rel = j - i - t * ATT_BLOCK
    n = jnp.abs(rel)
    large = jnp.full_like(n, 8)
    for thr in _BUCKET_THRESHOLDS:
        large = large + jnp.where(n >= thr, 1, 0)
    bucket = jnp.where(rel > 0, N_BUCKETS // 2, 0) + jnp.where(n < 8, n, large)
    far = rb_ref[FAR_BUCKET, h]
    val = jnp.zeros((ATT_BLOCK, ATT_BLOCK), F32)
    for b in range(N_BUCKETS):
        val = jnp.where(bucket == b, (rb_ref[b, h] - far) * LOG2E, val)
    visible = (j >> CHUNK_SHIFT) <= (i >> CHUNK_SHIFT) + t * ATT_BLOCK
    o_ref[0, 0, 0] = jnp.where(visible, val, -jnp.inf)


def _bias_tiles(rel_bias):
    return pl.pallas_call(
        _bias_kernel,
        grid=(2, 2, HEADS),
        in_specs=[pl.BlockSpec(memory_space=pltpu.SMEM)],
        out_specs=pl.BlockSpec((1, 1, 1, ATT_BLOCK, ATT_BLOCK), lambda o, t, h: (o, t, h, 0, 0)),
        out_shape=jax.ShapeDtypeStruct((2, 2, HEADS, ATT_BLOCK, ATT_BLOCK), F32),
        compiler_params=_cparams("arbitrary", "arbitrary", "arbitrary"),
        name="t5_bias_tiles",
    )(rel_bias)


def _in_proj_kernel(first_layer, transposed_v, x_ref, mod_ref, n1_ref, w_ref, wa2_ref, ba_ref, *refs):
    qg_ref, kg_ref, vg_ref, gg_ref, la_ref, qd_ref, kd_ref, vd_ref, kdb_ref, vdb_ref = refs[0 if first_layer else 2:]
    nb, tt, _ = x_ref.shape
    x = x_ref[...]
    ms = jnp.mean(x * x, axis=-1, keepdims=True)
    y = x * lax.rsqrt(ms + EPS) * n1_ref[...]
    mod = mod_ref[...]
    h = y * (1.0 + mod[:, 1:2, :]) + mod[:, 0:1, :]
    hb = h.reshape(nb * tt, D_MODEL).astype(BF16)
    proj = jnp.dot(hb, w_ref[...], preferred_element_type=F32)

    def put(ref, lo, width, scale=None):
        v = proj[:, lo:lo + width]
        if scale is not None:
            v = v * scale
        ref[...] = v.reshape(nb, tt, width).astype(ref.dtype)
        return lo + width

    lo = put(qg_ref, 0, GLA_QK, GLA_DK ** -0.5)
    lo = put(kg_ref, lo, GLA_QK)
    lo = put(vg_ref, lo, GLA_V)
    lo = put(gg_ref, lo, GLA_V)
    lo = put(qd_ref, lo, DIFF_QK, DIFF_DK ** -0.5 * LOG2E)

    def put_cache(ref, lo):
        for h in range(HEADS):
            slab = proj[:, lo + h * DIFF_DV:lo + (h + 1) * DIFF_DV]
            for b in range(nb):
                ref[0, b, pl.ds(h, tt, stride=HEADS), :] = slab[b * tt:(b + 1) * tt, :]
        if first_layer:
            ref[1:] = jnp.zeros((DEPTH - 1,) + ref.shape[1:], ref.dtype)

    put_cache(kd_ref, lo)
    lo = put(kdb_ref, lo, DIFF_QK)
    put_cache(vd_ref, lo)
    if transposed_v:
        pad_row = lax.broadcasted_iota(jnp.int32, (ATT_VROWS - DIFF_DV, nb * tt), 0)
        ones_row = jnp.where(pad_row == 0, 1.0, 0.0).astype(vdb_ref.dtype)
        for h in range(HEADS):
            vt = jnp.transpose(proj[:, lo + h * DIFF_DV:lo + (h + 1) * DIFF_DV]).astype(vdb_ref.dtype)
            vdb_ref[0, h * ATT_VROWS:h * ATT_VROWS + DIFF_DV, :] = vt
            vdb_ref[0, h * ATT_VROWS + DIFF_DV:(h + 1) * ATT_VROWS, :] = ones_row
        lo = lo + DIFF_V
    else:
        lo = put(vdb_ref, lo, DIFF_V)
    ar = proj[:, lo:lo + RANK_PAD].astype(BF16)
    z = jnp.dot(ar, wa2_ref[...], preferred_element_type=F32) + ba_ref[...]
    la = (jnp.minimum(z, 0.0) - jnp.log(1.0 + jnp.exp(-jnp.abs(z)))) * (1.0 / GLA_TAU)
    la_ref[...] = la.reshape(nb, tt, GLA_QK)


def _row_tiling(batch, seq, target):
    if seq >= target:
        assert seq % target == 0
        return 1, target
    nb = min(batch, max(1, target // seq))
    while batch % nb:
        nb -= 1
    return nb, seq


def _in_proj(x, mod, n1, w_all, wa2p, ba, layer, new_k, new_v, transposed_v):
    batch, seq, _ = x.shape
    first_layer = new_k is None
    nb, tt = (1, min(seq, 512)) if transposed_v else _row_tiling(batch, seq, 512)
    assert seq % tt == 0
    grid = (batch // nb, seq // tt)
    row = lambda w: pl.BlockSpec((nb, tt, w), lambda b, t: (b, t, 0))
    const = lambda shape: pl.BlockSpec(shape, lambda b, t: (0,) * len(shape))
    if first_layer:
        assert layer == 0
        cache = pl.BlockSpec((DEPTH, nb, tt * HEADS, DIFF_DV), lambda b, t: (0, b, t, 0))
    else:
        cache = pl.BlockSpec((1, nb, tt * HEADS, DIFF_DV), lambda b, t: (layer, b, t, 0))
    cache_shape = jax.ShapeDtypeStruct((DEPTH, batch, seq * HEADS, DIFF_DV), F32)
    sds = lambda w, dt: jax.ShapeDtypeStruct((batch, seq, w), dt)
    if transposed_v:
        vb_spec = pl.BlockSpec((1, HEADS * ATT_VROWS, tt), lambda b, t: (b, 0, t))
        vb_shape = jax.ShapeDtypeStruct((batch, HEADS * ATT_VROWS, seq), BF16)
    else:
        vb_spec, vb_shape = row(DIFF_V), sds(DIFF_V, BF16)
    out_specs = [row(GLA_QK), row(GLA_QK), row(GLA_V), row(GLA_V), row(GLA_QK), row(DIFF_QK),
                 cache, cache, row(DIFF_QK), vb_spec]
    out_shape = [sds(GLA_QK, F32), sds(GLA_QK, F32), sds(GLA_V, BF16), sds(GLA_V, F32), sds(GLA_QK, F32),
                 sds(DIFF_QK, BF16), cache_shape, cache_shape, sds(DIFF_QK, BF16), vb_shape]
    in_specs = [
        row(D_MODEL),
        pl.BlockSpec((nb, 6, D_MODEL), lambda b, t: (b, 0, 0)),
        const((1, 1, D_MODEL)),
        const((D_MODEL, PROJ_W)),
        const((RANK_PAD, GLA_QK)),
        const((1, GLA_QK)),
    ]
    args = [x, mod, n1.reshape(1, 1, D_MODEL), w_all, wa2p, ba.reshape(1, GLA_QK)]
    aliases = {}
    if not first_layer:
        aliases = {len(args): 6, len(args) + 1: 7}
        in_specs += [pl.BlockSpec(memory_space=pl.ANY)] * 2
        args += [new_k, new_v]
    return pl.pallas_call(
        functools.partial(_in_proj_kernel, first_layer, transposed_v),
        grid=grid,
        in_specs=in_specs,
        out_specs=out_specs,
        out_shape=out_shape,
        input_output_aliases=aliases,
        compiler_params=_cparams("arbitrary", "arbitrary"),
        name="in_projection",
    )(*args)


def _stack_heads(a, width):
    return jnp.concatenate([a[:, h * width:(h + 1) * width] for h in range(HEADS)], axis=0)


def _unstack_heads(a, rows):
    return jnp.concatenate([a[h * rows:(h + 1) * rows, :] for h in range(HEADS)], axis=1)


def _gla_kernel(q_ref, k_ref, la_ref, v_ref, gg_ref, s0_ref, g_ref, og_ref, sout_ref, s_scr, b_scr, v_scr):
    t = pl.program_id(1)
    rows = q_ref.shape[1]
    span = min(rows, GLA_FAST_SPAN)
    assert rows % span == 0 and span & (span - 1) == 0

    @pl.when(t == 0)
    def _():
        s_scr[...] = s0_ref[0]

    def masks(n):
        r = lax.broadcasted_iota(jnp.int32, (n, n), 0)
        c = lax.broadcasted_iota(jnp.int32, (n, n), 1)
        tril = jnp.where(c <= r, 1.0, 0.0).astype(BF16)
        rq = lax.broadcasted_iota(jnp.int32, (HEADS * n, GLA_QK), 0)
        cq = lax.broadcasted_iota(jnp.int32, (HEADS * n, GLA_QK), 1)
        head_lanes = (rq >> (n.bit_length() - 1)) == (cq >> DK_SHIFT)
        ra = lax.broadcasted_iota(jnp.int32, (HEADS * n, n), 0)
        ca = lax.broadcasted_iota(jnp.int32, (HEADS * n, n), 1)
        return tril, head_lanes, ca <= (ra & (n - 1))

    span_decay = jnp.sum(la_ref[0].reshape(rows // span, span, GLA_QK), axis=1)
    fast = jnp.logical_and(jnp.min(span_decay) >= -GLA_FAST_DECAY_LIMIT,
                           jnp.max(jnp.abs(k_ref[0])) <= GLA_FAST_KEY_LIMIT)

    def head_rows(a, h, n):
        return a[h * n:(h + 1) * n, :]

    def intra_fast(n, causal):
        def intra(base, q, k, v, b, qstack):
            kt = (k * jnp.exp(-b)).astype(BF16)
            a = lax.dot_general(qstack, kt, _NT, preferred_element_type=F32)
            a = jnp.where(causal, a, 0.0).astype(BF16)
            return jnp.concatenate(
                [jnp.dot(head_rows(a, h, n), v[:, h * GLA_DV:(h + 1) * GLA_DV], preferred_element_type=F32)
                 for h in range(HEADS)], axis=0)
        return intra

    def intra_exact(n):
        row_i = lax.broadcasted_iota(jnp.int32, (n, GLA_QK), 0)
        re = lax.broadcasted_iota(jnp.int32, (GLA_QK, GLA_V), 0)
        ce = lax.broadcasted_iota(jnp.int32, (GLA_QK, GLA_V), 1)
        expand = jnp.where((re >> DK_SHIFT) == (ce >> DV_SHIFT), 1.0, 0.0).astype(BF16)

        def intra(base, q, k, v, b, qstack):
            b_scr[...] = b
            v_scr[...] = v.astype(F32)

            def key_row(j, acc):
                kj = k_ref[0, pl.ds(base + j, 1), :]
                bj = b_scr[pl.ds(j, 1), :]
                vj = v_scr[pl.ds(j, 1), :]
                w = jnp.where(row_i >= j, jnp.exp(jnp.minimum(b - bj, 0.0)), 0.0) * q * kj
                return acc + jnp.dot(w.astype(BF16), expand, preferred_element_type=F32) * vj

            acc = lax.fori_loop(0, n, key_row, jnp.zeros((n, GLA_V), F32))
            return _stack_heads(acc, GLA_DV)
        return intra

    def process(n, base, s_old, tril, head_lanes, intra):
        q = q_ref[0, pl.ds(base, n), :]
        k = k_ref[0, pl.ds(base, n), :]
        la = la_ref[0, pl.ds(base, n), :]
        v = v_ref[0, pl.ds(base, n), :]
        la1 = la.astype(BF16)
        la2 = (la - la1.astype(F32)).astype(BF16)
        b = jnp.dot(tril, la1, preferred_element_type=F32) + jnp.dot(tril, la2, preferred_element_type=F32)
        b_last = b[n - 1:n, :]
        qt = (q * jnp.exp(b)).astype(BF16)
        kst = jnp.transpose(k * jnp.exp(b_last - b)).astype(BF16)
        qstack = jnp.where(head_lanes, jnp.concatenate([qt] * HEADS, axis=0), jnp.zeros((), BF16))
        o = jnp.dot(qstack, s_old.astype(BF16), preferred_element_type=F32)
        o = o + intra(base, q, k, v, b, qstack)
        gain = jnp.concatenate([jnp.broadcast_to(g_ref[h:h + 1, :], (n, GLA_DV)) for h in range(HEADS)], axis=0)
        og = o * lax.rsqrt(jnp.mean(o * o, axis=-1, keepdims=True) + EPS) * gain
        gate = _stack_heads(gg_ref[0, pl.ds(base, n), :], GLA_DV)
        og = og * (gate * _sigmoid(gate))
        og_ref[0, pl.ds(base, n), :] = _unstack_heads(og, n).astype(og_ref.dtype)
        ds = jnp.concatenate(
            [jnp.dot(head_rows(kst, h, GLA_DK), v[:, h * GLA_DV:(h + 1) * GLA_DV], preferred_element_type=F32)
             for h in range(HEADS)], axis=0)
        decay = jnp.transpose(jnp.broadcast_to(jnp.exp(b_last), (GLA_DV, GLA_QK)))
        return decay * s_old + ds

    def fast_step(s):
        tril, head_lanes, causal = masks(span)
        for i in range(rows // span):
            s = process(span, i * span, s, tril, head_lanes, intra_fast(span, causal))
        return s

    def exact_step(s):
        tril, head_lanes, _ = masks(CHUNK)
        intra = intra_exact(CHUNK)
        return lax.fori_loop(
            0, rows // CHUNK,
            lambda c, s: process(CHUNK, pl.multiple_of(c * CHUNK, CHUNK), s, tril, head_lanes, intra), s)

    s_scr[...] = lax.cond(fast, fast_step, exact_step, s_scr[...])

    @pl.when(t == pl.num_programs(1) - 1)
    def _():
        sout_ref[0] = s_scr[...]


def _gla(qg, kg, la, vg, gg, s0, gain):
    batch, seq, _ = qg.shape
    rows = min(seq, 512)
    assert seq % rows == 0 and rows % CHUNK == 0
    grid = (batch, seq // rows)
    row = lambda w: pl.BlockSpec((1, rows, w), lambda b, t: (b, t, 0))
    state = pl.BlockSpec((1, GLA_QK, GLA_DV), lambda b, t: (b, 0, 0))
    return pl.pallas_call(
        _gla_kernel,
        grid=grid,
        in_specs=[row(GLA_QK), row(GLA_QK), row(GLA_QK), row(GLA_V), row(GLA_V), state,
                  pl.BlockSpec((HEADS, GLA_DV), lambda b, t: (0, 0))],
        out_specs=[row(GLA_V), state],
        out_shape=[jax.ShapeDtypeStruct((batch, seq, GLA_V), BF16),
                   jax.ShapeDtypeStruct((batch, GLA_QK, GLA_DV), F32)],
        scratch_shapes=[pltpu.VMEM((GLA_QK, GLA_DV), F32),
                        pltpu.VMEM((CHUNK, GLA_QK), F32),
                        pltpu.VMEM((CHUNK, GLA_V), F32)],
        compiler_params=_cparams("arbitrary", "arbitrary"),
        name="gla_chunks",
    )(qg, kg, la, vg, gg, s0, gain)


def _split_maps(qh):
    lane = lax.broadcasted_iota(jnp.int32, qh.shape, 1)
    zero = jnp.zeros_like(qh)
    return jnp.concatenate([jnp.where(lane < DIFF_DK, qh, zero), jnp.where(lane >= DIFF_DK, qh, zero)], axis=0)


def _diff_lambda(lam_ref, lam_init):
    lv = lam_ref[...]
    s1 = jnp.sum(lv[0:1, :] * lv[1:2, :], axis=1, keepdims=True)
    s2 = jnp.sum(lv[2:3, :] * lv[3:4, :], axis=1, keepdims=True)
    return jnp.exp(s1) - jnp.exp(s2) + lam_init


_NT = (((1,), (1,)), ((), ()))


def _prompt_scores(k_ref, q2, base, size, dst):
    for h in range(HEADS):
        cols = slice(h * 2 * DIFF_DK, (h + 1) * 2 * DIFF_DK)
        dst[h][0:size, :] = lax.dot_general(k_ref[0, pl.ds(base, size), cols], q2[h], _NT,
                                            preferred_element_type=F32)


def _prompt_softmax_step(src, size, vt_ref, base, bias_t_ref, tiles, m_ref, acc_ref):
    for h in range(HEADS):
        s = src[h][0:size, :]
        if tiles is not None:
            parts = []
            for i, tile in enumerate(tiles):
                part = s[i * ATT_BLOCK:(i + 1) * ATT_BLOCK, :]
                if tile is not None:
                    part = part + jnp.concatenate([bias_t_ref[tile, h]] * 2, axis=1)
                parts.append(part)
            s = jnp.concatenate(parts, axis=0)
        m_prev = m_ref[h]
        m_next = jnp.maximum(m_prev, jnp.max(s, axis=0, keepdims=True))
        p = jnp.exp2(s - m_next)
        alpha = jnp.exp2(m_prev - m_next)
        vtb = vt_ref[0, h * ATT_VROWS:(h + 1) * ATT_VROWS, pl.ds(base, size)]
        acc_ref[h] = alpha * acc_ref[h] + jnp.dot(vtb, p.astype(BF16), preferred_element_type=F32)
        m_ref[h] = m_next


def _attn_prompt_kernel(lam_init, q_ref, k_ref, vt_ref, bias_ref, lam_ref, g_ref, o_ref, m_ref, acc_ref, *scr):
    s_scr = scr
    qi = pl.program_id(1)
    tq = q_ref.shape[1]
    lam = _diff_lambda(lam_ref, lam_init)
    m_ref[...] = jnp.full(m_ref.shape, -jnp.inf, F32)
    acc_ref[...] = jnp.zeros(acc_ref.shape, F32)
    q2 = [_split_maps(q_ref[0, :, h * 2 * DIFF_DK:(h + 1) * 2 * DIFF_DK]) for h in range(HEADS)]
    nfar = jnp.maximum(qi - 1, 0)
    npair = nfar // 2
    odd = nfar - 2 * npair

    def far_step(base, size):
        _prompt_scores(k_ref, q2, base, size, s_scr)
        _prompt_softmax_step(s_scr, size, vt_ref, base, bias_ref, None, m_ref, acc_ref)

    def far_quad(i, carry):
        far_step(pl.multiple_of(i * ATT_QUAD, ATT_QUAD), ATT_QUAD)
        return carry

    nquad = npair // 2
    lax.fori_loop(0, nquad, far_quad, 0)

    @pl.when(npair - 2 * nquad == 1)
    def _():
        far_step(pl.multiple_of(nquad * ATT_QUAD, ATT_PAIR), ATT_PAIR)
    tail = pl.multiple_of(npair * ATT_PAIR, ATT_BLOCK)

    def tail_step(tiles):
        def run(_):
            size = len(tiles) * ATT_BLOCK
            _prompt_scores(k_ref, q2, tail, size, s_scr)
            _prompt_softmax_step(s_scr, size, vt_ref, tail, bias_ref, tiles, m_ref, acc_ref)
            return 0
        return run

    lax.switch(jnp.where(qi == 0, 0, 1 + odd), [tail_step((0,)), tail_step((1, 0)), tail_step((None, 1, 0))], 0)
    for h in range(HEADS):
        a = acc_ref[h]
        out = a[:DIFF_DV, :] * (1.0 / a[DIFF_DV:DIFF_DV + 1, :])
        od = out[:, :tq] - lam * out[:, tq:]
        od = od * lax.rsqrt(jnp.mean(od * od, axis=0, keepdims=True) + EPS)
        vcols = slice(h * DIFF_DV, (h + 1) * DIFF_DV)
        od = jnp.transpose(od) * g_ref[:, vcols] * (1.0 - lam_init)
        o_ref[0, :, vcols] = od.astype(o_ref.dtype)


def _attn_prompt(qd, kdb, vtb, bias_t, lam_vecs, g_row, lam_init):
    batch, seq, _ = qd.shape
    tq = ATT_BLOCK
    assert seq % tq == 0 and seq >= ATT_PAIR
    return pl.pallas_call(
        functools.partial(_attn_prompt_kernel, lam_init),
        grid=(batch, seq // tq),
        in_specs=[
            pl.BlockSpec((1, tq, DIFF_QK), lambda b, i: (b, i, 0)),
            pl.BlockSpec((1, seq, DIFF_QK), lambda b, i: (b, 0, 0)),
            pl.BlockSpec((1, HEADS * ATT_VROWS, seq), lambda b, i: (b, 0, 0)),
            pl.BlockSpec((2, HEADS, ATT_BLOCK, ATT_BLOCK), lambda b, i: (0, 0, 0, 0)),
            pl.BlockSpec((4, DIFF_DK), lambda b, i: (0, 0)),
            pl.BlockSpec((1, DIFF_V), lambda b, i: (0, 0)),
        ],
        out_specs=pl.BlockSpec((1, tq, DIFF_V), lambda b, i: (b, i, 0)),
        out_shape=jax.ShapeDtypeStruct((batch, seq, DIFF_V), BF16),
        scratch_shapes=[pltpu.VMEM((HEADS, 1, 2 * tq), F32), pltpu.VMEM((HEADS, ATT_VROWS, 2 * tq), F32)]
                       + [pltpu.VMEM((ATT_QUAD, 2 * tq), F32)] * HEADS,
        compiler_params=_cparams("arbitrary", "arbitrary"),
        name="diff_attention_prompt",
    )(qd, kdb, vtb, bias_t, lam_vecs, g_row)


def _with_ones_column(v):
    lane = lax.broadcasted_iota(jnp.int32, v.shape, 1)
    return jnp.concatenate([v, jnp.where(lane == 0, 1.0, 0.0).astype(v.dtype)], axis=1)


def _sample_softmax_step(s, v_aug, m_ref, acc_ref, h):
    m_prev = m_ref[h]
    m_next = jnp.maximum(m_prev, jnp.max(s, axis=1, keepdims=True))
    p = jnp.exp2(s - m_next[:, :1])
    alpha = jnp.exp2(m_prev - m_next)
    acc_ref[h] = (jnp.concatenate([alpha, alpha], axis=1) * acc_ref[h]
                  + jnp.dot(p.astype(BF16), v_aug, preferred_element_type=F32))
    m_ref[h] = m_next


def _attn_sample_kernel(lam_init, q_ref, ck_ref, cv_ref, nk_ref, nv_ref, bias_ref, lam_ref, g_ref, o_ref,
                        m_ref, acc_ref, *s_scr):
    step = pl.program_id(1)
    nstep = pl.num_programs(1)
    tq = q_ref.shape[1]
    tk = ck_ref.shape[2] // HEADS

    def head_rows(ref, h):
        return ref[0, 0, pl.ds(h, tk, stride=HEADS), :].astype(BF16)

    @pl.when(step == 0)
    def _():
        m_ref[...] = jnp.full(m_ref.shape, -jnp.inf, F32)
        acc_ref[...] = jnp.zeros(acc_ref.shape, F32)

    q2 = [_split_maps(q_ref[0, :, h * 2 * DIFF_DK:(h + 1) * 2 * DIFF_DK]) for h in range(HEADS)]
    for h in range(HEADS):
        s_scr[h][...] = lax.dot_general(q2[h], head_rows(ck_ref, h), _NT, preferred_element_type=F32)

    def cache_block(near):
        for h in range(HEADS):
            s = s_scr[h][...]
            if near:
                bias = jnp.concatenate([bias_ref[1, h, :tq, :]] * 2, axis=0)
                s = jnp.concatenate([s[:, :tk - ATT_BLOCK], s[:, tk - ATT_BLOCK:] + bias], axis=1)
            _sample_softmax_step(s, _with_ones_column(head_rows(cv_ref, h)), m_ref, acc_ref, h)

    @pl.when(step < nstep - 1)
    def _():
        cache_block(False)

    @pl.when(step == nstep - 1)
    def _():
        cache_block(True)
        lam = _diff_lambda(lam_ref, lam_init)
        for h in range(HEADS):
            cols = slice(h * 2 * DIFF_DK, (h + 1) * 2 * DIFF_DK)
            vcols = slice(h * DIFF_DV, (h + 1) * DIFF_DV)
            s = lax.dot_general(q2[h], nk_ref[0, :, cols], _NT, preferred_element_type=F32)
            s = s + jnp.concatenate([bias_ref[0, h, :tq, :tq]] * 2, axis=0)
            _sample_softmax_step(s, _with_ones_column(nv_ref[0, :, vcols]), m_ref, acc_ref, h)
            a = acc_ref[h]
            out = a[:, :DIFF_DV] * (1.0 / a[:, DIFF_DV:DIFF_DV + 1])
            od = out[:tq, :] - lam * out[tq:, :]
            od = od * lax.rsqrt(jnp.mean(od * od, axis=-1, keepdims=True) + EPS) * g_ref[:, vcols]
            o_ref[0, :, vcols] = (od * (1.0 - lam_init)).astype(o_ref.dtype)


def _attn_sample(qd, cache_k, cache_v, layer, kdb, vdb, bias, lam_vecs, g_row, lam_init):
    batch, tq, _ = qd.shape
    past = cache_k.shape[2] // HEADS
    assert tq == CHUNK and past % CHUNK == 0, "sample queries must be one chunk, appended at a chunk boundary"
    tk = min(past, SAMPLE_KEY_BLOCK)
    assert past % tk == 0 and tk % ATT_BLOCK == 0
    new = lambda w: pl.BlockSpec((1, tq, w), lambda b, s: (b, 0, 0))
    blk = pl.BlockSpec((1, 1, tk * HEADS, DIFF_DV), lambda b, s: (layer, b, s, 0))
    return pl.pallas_call(
        functools.partial(_attn_sample_kernel, lam_init),
        grid=(batch, past // tk),
        in_specs=[
            new(DIFF_QK), blk, blk, new(DIFF_QK), new(DIFF_V),
            pl.BlockSpec((2, HEADS, ATT_BLOCK, ATT_BLOCK), lambda b, s: (0, 0, 0, 0)),
            pl.BlockSpec((4, DIFF_DK), lambda b, s: (0, 0)),
            pl.BlockSpec((1, DIFF_V), lambda b, s: (0, 0)),
        ],
        out_specs=new(DIFF_V),
        out_shape=jax.ShapeDtypeStruct((batch, tq, DIFF_V), BF16),
        scratch_shapes=[pltpu.VMEM((HEADS, 2 * tq, DIFF_DV), F32), pltpu.VMEM((HEADS, 2 * tq, 2 * DIFF_DV), F32)]
                       + [pltpu.VMEM((2 * tq, tk), F32)] * HEADS,
        compiler_params=_cparams("arbitrary", "arbitrary"),
        name="diff_attention_sample",
    )(qd, cache_k, cache_v, kdb, vdb, bias, lam_vecs, g_row)


def _out_mlp_kernel(final, x_ref, og_ref, od_ref, mod_ref, n2_ref, fg_ref, wo_ref, wu_ref, wd_ref, o_ref):
    nb, tt, _ = x_ref.shape
    rows = nb * tt
    mod = mod_ref[...]
    o = jnp.concatenate([og_ref[...].reshape(rows, GLA_V), od_ref[...].reshape(rows, DIFF_V)], axis=1)
    attn = jnp.dot(o, wo_ref[...], preferred_element_type=F32).reshape(nb, tt, D_MODEL)
    x1 = x_ref[...] + mod[:, 2:3, :] * attn
    ms = jnp.mean(x1 * x1, axis=-1, keepdims=True)
    h2 = x1 * lax.rsqrt(ms + EPS) * n2_ref[...]
    h2 = (h2 * (1.0 + mod[:, 4:5, :]) + mod[:, 3:4, :]).reshape(rows, D_MODEL).astype(BF16)
    ff = jnp.zeros((rows, D_MODEL), F32)
    for c in range(D_FF // D_MODEL):
        cs = slice(c * D_MODEL, (c + 1) * D_MODEL)
        u = jnp.maximum(jnp.dot(h2, wu_ref[:, cs], preferred_element_type=F32), 0.0)
        ff = ff + jnp.dot((u * u).astype(BF16), wd_ref[cs, :], preferred_element_type=F32)
    x2 = x1 + mod[:, 5:6, :] * ff.reshape(nb, tt, D_MODEL)
    if final:
        ms2 = jnp.mean(x2 * x2, axis=-1, keepdims=True)
        x2 = x2 * lax.rsqrt(ms2 + EPS) * fg_ref[...]
    o_ref[...] = x2


def _out_mlp(x, og, od, mod, n2, final_g, w_out, w_up, w_down, final):
    batch, seq, _ = x.shape
    nb, tt = _row_tiling(batch, seq, 512)
    row = lambda w: pl.BlockSpec((nb, tt, w), lambda b, t: (b, t, 0))
    const = lambda shape: pl.BlockSpec(shape, lambda b, t: (0,) * len(shape), pipeline_mode=pl.Buffered(1))
    return pl.pallas_call(
        functools.partial(_out_mlp_kernel, final),
        grid=(batch // nb, seq // tt),
        in_specs=[
            row(D_MODEL), row(GLA_V), row(DIFF_V),
            pl.BlockSpec((nb, 6, D_MODEL), lambda b, t: (b, 0, 0)),
            const((1, 1, D_MODEL)), const((1, 1, D_MODEL)),
            const((D_MODEL, D_MODEL)), const((D_MODEL, D_FF)), const((D_FF, D_MODEL)),
        ],
        out_specs=row(D_MODEL),
        out_shape=jax.ShapeDtypeStruct((batch, seq, D_MODEL), F32),
        compiler_params=_cparams("arbitrary", "arbitrary"),
        name="out_proj_mlp",
    )(x, og, od, mod, n2.reshape(1, 1, D_MODEL), final_g.reshape(1, 1, D_MODEL), w_out, w_up, w_down)


def _pack_w_in(w_in_l):
    a = 2 * GLA_QK + 2 * GLA_V
    main = jnp.concatenate([w_in_l[:, :a], w_in_l[:, a + GLA_RANK:]], axis=1)
    ar = jnp.pad(w_in_l[:, a:a + GLA_RANK], ((0, 0), (0, RANK_PAD - GLA_RANK)))
    return jnp.concatenate([main, ar], axis=1).astype(BF16)


def _trunk(x, mod_all, state0, cache_k, cache_v, params, bias):
    batch, seq, _ = x.shape
    prompt = cache_k is None
    if not prompt:
        cache_k = cache_k.reshape(DEPTH, batch, -1, DIFF_DV)
        cache_v = cache_v.reshape(DEPTH, batch, -1, DIFF_DV)
    new_k = new_v = None
    ss = []
    for l in range(DEPTH):
        p = params[l]
        lam_init = 0.8 - 0.6 * math.exp(-0.3 * l)
        mod = mod_all[l]
        qg, kg, vg, gg, la, qd, new_k, new_v, kdb, vdb = _in_proj(
            x, mod, p["n1"], p["w_all"], p["wa2p"], p["ba"], l, new_k, new_v, transposed_v=prompt)
        if state0 is None:
            s0 = jnp.zeros((batch, GLA_QK, GLA_DV), F32)
        else:
            s0 = state0[l].reshape(batch, GLA_QK, GLA_DV)
        og, s_new = _gla(qg, kg, la, vg, gg, s0, p["gla_g"])
        if prompt:
            od = _attn_prompt(qd, kdb, vdb, bias[1], p["lam_vecs"], p["diff_g"], lam_init)
        else:
            od = _attn_sample(qd, cache_k, cache_v, l, kdb, vdb, bias[0], p["lam_vecs"], p["diff_g"], lam_init)
        x = _out_mlp(x, og, od, mod, p["n2"], p["final_g"], p["w_out"], p["w_up"], p["w_down"], l == DEPTH - 1)
        ss.append(s_new.reshape(batch, HEADS, GLA_DK, GLA_DV))
    cache_shape = (DEPTH, batch, seq, HEADS, DIFF_DV)
    return x, new_k.reshape(cache_shape), new_v.reshape(cache_shape), jnp.stack(ss)


def kernel(x_prompt, x_sample, cache_k, cache_v, state_gla, c_prompt, c_sample, w_mod, b_mod, norm1_g, w_in, w_a2, b_a, gla_norm_g, lam_q1, lam_k1, lam_q2, lam_k2, diff_norm_g, rel_bias, w_out, norm2_g, w_up, w_down, final_g):
    nbp = x_prompt.shape[0]
    mod_all = _modulation(jnp.concatenate([c_prompt, c_sample], axis=0), w_mod, b_mod)
    mod_all = mod_all.reshape(DEPTH, -1, 6, D_MODEL)
    bias = _bias_tiles(rel_bias)
    params = []
    for l in range(DEPTH):
        params.append(dict(
            n1=norm1_g[l], n2=norm2_g[l], final_g=final_g,
            w_all=_pack_w_in(w_in[l]),
            wa2p=jnp.pad(w_a2[l], ((0, RANK_PAD - GLA_RANK), (0, 0))).astype(BF16),
            ba=b_a[l],
            gla_g=gla_norm_g[l],
            lam_vecs=jnp.stack([lam_q1[l], lam_k1[l], lam_q2[l], lam_k2[l]]),
            diff_g=diff_norm_g[l].reshape(1, DIFF_V),
            w_out=w_out[l].astype(BF16), w_up=w_up[l].astype(BF16), w_down=w_down[l].astype(BF16),
        ))
    y_p, k_p, v_p, s_p = _trunk(x_prompt, mod_all[:, :nbp], None, None, None, params, bias)
    y_s, k_s, v_s, s_s = _trunk(x_sample, mod_all[:, nbp:], state_gla, cache_k, cache_v, params, bias)
    return (y_p, y_s, k_p, v_p, s_p, k_s, v_s, s_s)
```
